```python
import math
import jax, jax.numpy as jnp
from jax import lax
import numpy as np

D_MODEL = 2048
BATCH = 4
SEQ = 4096
DEPTH = 1

CHUNK = 64
Q_BLOCK = 128
EPS = 1e-6

A_WIDTH = D_MODEL // 2
A_HEAD_DIM = 128
A_HEADS = A_WIDTH // A_HEAD_DIM

B_WIDTH = D_MODEL // 2
B_HEADS = 8
B_VDIM = B_WIDTH // B_HEADS
B_QKDIM = B_VDIM // 2

SPLIT_SIZES = [A_WIDTH, A_WIDTH, A_WIDTH, A_WIDTH,
               B_WIDTH, B_WIDTH, B_WIDTH, B_WIDTH,
               D_MODEL, D_MODEL]
N_IN = sum(SPLIT_SIZES)

kernel_name = "hgrn2_diffattn_gated_hybrid"


def rms_norm(x, gain):
    xf = x.astype(jnp.float32)
    y = xf * lax.rsqrt(jnp.mean(xf * xf, axis=-1, keepdims=True) + EPS)
    return (y * gain.astype(jnp.float32)).astype(x.dtype)


def hgrn2_mixer(q, f_logit, inp, lb, out_gain):
    Bsz, T, _ = q.shape
    n = T // CHUNK
    f32 = jnp.float32
    z = f_logit.astype(f32)
    lbf = lb.astype(f32)
    q = jax.nn.silu(q.astype(f32))
    log_f = jnp.log(lbf + (1.0 - lbf) * jax.nn.sigmoid(z))
    k = (1.0 - lbf) * jax.nn.sigmoid(-z)
    v = inp.astype(f32)

    def split(t):
        return t.reshape(Bsz, n, CHUNK, A_HEADS, A_HEAD_DIM).transpose(1, 0, 3, 2, 4)

    causal = jnp.tril(jnp.ones((CHUNK, CHUNK), dtype=bool))

    def step(S, xs):
        qc, kc, vc, lfc = xs
        b = jnp.cumsum(lfc, axis=2)
        diff = b[:, :, :, None, :] - b[:, :, None, :, :]
        decay = jnp.where(causal[None, None, :, :, None],
                          jnp.exp(jnp.minimum(diff, 0.0)), 0.0)
        scores = jnp.einsum('bhtd,bhsd,bhtsd->bhts', qc, kc, decay)
        intra = jnp.einsum('bhts,bhsv->bhtv', scores, vc)
        inter = jnp.einsum('bhtd,bhdv->bhtv', qc * jnp.exp(b), S)
        b_last = b[:, :, -1:, :]
        k_dec = kc * jnp.exp(b_last - b)
        S_new = S * jnp.exp(b_last[:, :, 0, :, None]) + jnp.einsum('bhsd,bhsv->bhdv', k_dec, vc)
        return S_new, intra + inter

    S0 = jnp.zeros((Bsz, A_HEADS, A_HEAD_DIM, A_HEAD_DIM), f32)
    _, out = lax.scan(step, S0, (split(q), split(k), split(v), split(log_f)))
    out = out.transpose(1, 0, 3, 2, 4).reshape(Bsz, T, A_HEADS, A_HEAD_DIM)
    out = rms_norm(out, out_gain)
    return out.reshape(Bsz, T, A_WIDTH)


def diff_attention(q, k, v, q_gain, k_gain, lam_vecs, subln_gain, lambda_init):
    Bsz, T, _ = q.shape
    f32 = jnp.float32
    q = rms_norm(q.reshape(Bsz, T, B_HEADS, 2, B_QKDIM), q_gain).astype(f32)
    k = rms_norm(k.reshape(Bsz, T, B_HEADS, 2, B_QKDIM), k_gain).astype(f32)
    q = q.transpose(0, 2, 1, 3, 4)
    k = k.transpose(0, 2, 1, 3, 4)
    v = v.astype(f32).reshape(Bsz, T, B_HEADS, B_VDIM).transpose(0, 2, 1, 3)
    lv = lam_vecs.astype(f32)
    lam = (jnp.exp(jnp.sum(lv[0] * lv[1])) - jnp.exp(jnp.sum(lv[2] * lv[3])) + lambda_init)
    slopes = jnp.exp2(-8.0 * (jnp.arange(B_HEADS, dtype=f32) + 1.0) / B_HEADS)
    scale = B_QKDIM ** -0.5
    outs = []
    for blk in range(T // Q_BLOCK):
        q0 = blk * Q_BLOCK
        kend = q0 + Q_BLOCK
        qb = q[:, :, q0:kend]
        kb = k[:, :, :kend]
        s = jnp.einsum('bhqcd,bhkcd->bhcqk', qb, kb) * scale
        tq = q0 + jnp.arange(Q_BLOCK)
        tk = jnp.arange(kend)
        dist = jnp.abs(tq[:, None] - tk[None, :]).astype(f32)
        allowed = (tk[None, :] // CHUNK) <= (tq[:, None] // CHUNK)
        s = s - (slopes[:, None, None] * dist)[None, :, None]
        s = jnp.where(allowed[None, None, None], s, -jnp.inf)
        p = jax.nn.softmax(s, axis=-1)
        w = p[:, :, 0] - lam * p[:, :, 1]
        outs.append(jnp.einsum('bhqk,bhkv->bhqv', w, v[:, :, :kend]))
    o = jnp.concatenate(outs, axis=2)
    o = rms_norm(o, subln_gain) * (1.0 - lambda_init)
    return o.transpose(0, 2, 1, 3).reshape(Bsz, T, B_WIDTH)


def setup_inputs(seed: int = 0) -> dict:
    key = jax.random.key(seed)
    ks = jax.random.split(key, 12)
    f32 = jnp.float32
    nrm = lambda k, s, sc: jax.random.normal(k, s, f32) * sc
    return {
        "x": nrm(ks[0], (BATCH, SEQ, D_MODEL), 1.0),
        "norm_w": 1.0 + nrm(ks[1], (DEPTH, D_MODEL), 0.02),
        "w_in": nrm(ks[2], (DEPTH, D_MODEL, N_IN), D_MODEL ** -0.5),
        "a_lower_bound": nrm(ks[3], (DEPTH + 1, A_WIDTH), 0.1),
        "a_out_norm": 1.0 + nrm(ks[4], (DEPTH, A_HEAD_DIM), 0.02),
        "b_q_norm": 1.0 + nrm(ks[5], (DEPTH, B_QKDIM), 0.02),
        "b_k_norm": 1.0 + nrm(ks[6], (DEPTH, B_QKDIM), 0.02),
        "b_lambda": nrm(ks[7], (DEPTH, 4, B_QKDIM), 0.1),
        "b_subln": 1.0 + nrm(ks[8], (DEPTH, B_VDIM), 0.02),
        "w_branch_a": nrm(ks[9], (DEPTH, A_WIDTH, D_MODEL), A_WIDTH ** -0.5),
        "w_branch_b": nrm(ks[10], (DEPTH, B_WIDTH, D_MODEL), B_WIDTH ** -0.5),
        "w_out": nrm(ks[11], (DEPTH, D_MODEL, D_MODEL), D_MODEL ** -0.5),
    }


def reference(x, norm_w, w_in, a_lower_bound, a_out_norm, b_q_norm, b_k_norm,
              b_lambda, b_subln, w_branch_a, w_branch_b, w_out):
    split_idx = [int(v) for v in np.cumsum(SPLIT_SIZES)[:-1]]
    lb_all = jnp.cumsum(jax.nn.softmax(a_lower_bound.astype(jnp.float32), axis=0), axis=0)
    for l in range(DEPTH):
        h = rms_norm(x, norm_w[l])
        proj = jnp.einsum('btd,dn->btn', h, w_in[l])
        (a_q, a_f, a_i, a_g, b_q, b_k, b_v, b_g,
         gate_a, gate_b) = jnp.split(proj, split_idx, axis=-1)
        ya = hgrn2_mixer(a_q, a_f, a_i, lb_all[l], a_out_norm[l]).astype(x.dtype) * jax.nn.silu(a_g)
        lambda_init = 0.8 - 0.6 * math.exp(-0.3 * l)
        yb = diff_attention(b_q, b_k, b_v, b_q_norm[l], b_k_norm[l], b_lambda[l],
                            b_subln[l], lambda_init).astype(x.dtype) * jax.nn.silu(b_g)
        ya = jnp.einsum('btw,wd->btd', ya, w_branch_a[l])
        yb = jnp.einsum('btw,wd->btd', yb, w_branch_b[l])
        mixed = jax.nn.sigmoid(gate_a) * ya + jax.nn.sigmoid(gate_b) * yb
        x = x + jnp.einsum('btd,de->bte', mixed, w_out[l])
    return x
```

```python
import functools
import math

import jax
import jax.numpy as jnp
from jax import lax
from jax.experimental import pallas as pl
from jax.experimental.pallas import tpu as pltpu

F32 = jnp.float32
BF16 = jnp.bfloat16

EPS = 1e-6
CHUNK = 64
HEAD = 128
HEADS = 8
QK = 64
WIDTH = HEADS * HEAD
SUB = 8
LAMBDA_INIT = 0.8 - 0.6 * math.exp(-0.3 * 0)
NEG = -1e30

SEG_AQ, SEG_AF, SEG_AI, SEG_AG, SEG_BQ, SEG_BK, SEG_BV, SEG_BG = (s * HEADS for s in range(8))
SEG_GATE_A, SEG_GATE_B = 4, 5

VMEM_LIMIT = 56 * 1024 * 1024


def _dot(a, b, dims):
    return lax.dot_general(a, b, (dims, ((), ())), preferred_element_type=F32)


NN = ((1,), (0,))
NT = ((1,), (1,))
TN = ((0,), (0,))


def _in_proj_kernel(x_ref, nw_ref, w_ref, o_ref, h_ref):
    @pl.when(pl.program_id(1) == 0)
    def _():
        x = x_ref[...]
        ms = jnp.mean(x * x, axis=-1, keepdims=True)
        h_ref[...] = (x * lax.rsqrt(ms + EPS) * nw_ref[...]).astype(BF16)

    o_ref[...] = jnp.dot(h_ref[...], w_ref[...], preferred_element_type=F32)


def _in_proj(x2, norm_w, w_in_bf16, tm=1024, tn=1024):
    m, d = x2.shape
    n = w_in_bf16.shape[1]
    return pl.pallas_call(
        _in_proj_kernel,
        grid=(m // tm, n // tn),
        in_specs=[
            pl.BlockSpec((tm, d), lambda i, j: (i, 0)),
            pl.BlockSpec((1, d), lambda i, j: (0, 0)),
            pl.BlockSpec((d, tn), lambda i, j: (0, j)),
        ],
        out_specs=pl.BlockSpec((tm, tn), lambda i, j: (i, j)),
        out_shape=jax.ShapeDtypeStruct((m, n), F32),
        scratch_shapes=[pltpu.VMEM((tm, d), BF16)],
        compiler_params=pltpu.CompilerParams(
            dimension_semantics=("arbitrary", "arbitrary"), vmem_limit_bytes=VMEM_LIMIT),
        name="in_proj",
    )(x2, norm_w, w_in_bf16)


def _hgrn_kernel(alb_ref, gain_ref, aq_ref, af_ref, ai_ref, ag_ref, o_ref, st_ref, *, tt):
    @pl.when(pl.program_id(2) == 0)
    def _():
        st_ref[...] = jnp.zeros_like(st_ref)

    a = alb_ref[...]
    a0, a1 = a[0:1, :], a[1:2, :]
    amax = jnp.maximum(a0, a1)
    e0, e1 = jnp.exp(a0 - amax), jnp.exp(a1 - amax)
    lb = e0 / (e0 + e1)
    one_m_lb = 1.0 - lb
    gain = gain_ref[...]

    row = lax.broadcasted_iota(jnp.int32, (CHUNK, CHUNK), 0)
    col = lax.broadcasted_iota(jnp.int32, (CHUNK, CHUNK), 1)
    tril = (row >= col).astype(BF16)
    same32 = (row >> 5) == (col >> 5)
    same16 = (row >> 4) == (col >> 4)
    rowc = lax.broadcasted_iota(jnp.int32, (CHUNK, HEAD), 0)
    nblk = CHUNK // SUB
    sub3 = lax.broadcasted_iota(jnp.int32, (nblk, SUB, HEAD), 1)
    lane3 = lax.broadcasted_iota(jnp.int32, (nblk, SUB, CHUNK), 2)
    blk3 = lax.broadcasted_iota(jnp.int32, (nblk, SUB, CHUNK), 0)

    def level(b, q, k, blk):
        half = blk // 2
        mids = [jnp.broadcast_to(b[j * blk + half:j * blk + half + 1, :], (blk, HEAD))
                for j in range(CHUNK // blk)]
        mid = mids[0] if len(mids) == 1 else jnp.concatenate(mids, axis=0)
        upper = (rowc & (blk - 1)) >= half
        d = b - mid
        e = jnp.exp(jnp.where(upper, d, -d))
        ql = jnp.where(upper, q * e, 0.0).astype(BF16)
        kl = jnp.where(upper, 0.0, k * e).astype(BF16)
        return _dot(kl, ql, NT)

    def chunk(c, carry):
        r = pl.ds(pl.multiple_of(c * CHUNK, CHUNK), CHUNK)
        z = af_ref[r, :]
        qa = aq_ref[r, :]
        v = ai_ref[r, :].astype(BF16)
        g = ag_ref[r, :]

        sig = jax.nn.sigmoid(z)
        f = lb + one_m_lb * sig
        lf = jnp.log(f)
        k = one_m_lb * (1.0 - sig)
        q = qa * jax.nn.sigmoid(qa)

        hi = lf.astype(BF16)
        r1 = lf - hi.astype(F32)
        mid_ = r1.astype(BF16)
        lo = (r1 - mid_.astype(F32)).astype(BF16)
        b = _dot(tril, hi, NN) + _dot(tril, mid_, NN) + _dot(tril, lo, NN)
        b_last = b[CHUNK - 1:CHUNK, :]

        st = st_ref[...]
        inter = _dot((q * jnp.exp(b)).astype(BF16), st.astype(BF16), NT)
        kd = (k * jnp.exp(b_last - b)).astype(BF16)
        st_ref[...] = st * jnp.exp(b_last) + _dot(v, kd, TN)

        sc = level(b, q, k, 64)
        sc = sc + jnp.where(same32, level(b, q, k, 32), 0.0)
        sc = sc + jnp.where(same16, level(b, q, k, 16), 0.0)

        k3 = k.reshape(nblk, SUB, HEAD)
        q3 = q.reshape(nblk, SUB, HEAD)
        f3 = f.reshape(nblk, SUB, HEAD)
        gk = jnp.zeros((nblk, SUB, HEAD), F32)
        pt = jnp.zeros((nblk, SUB, CHUNK), F32)
        for tl in range(SUB):
            fb = jnp.broadcast_to(f3[:, tl:tl + 1, :], (nblk, SUB, HEAD))
            qb = jnp.broadcast_to(q3[:, tl:tl + 1, :], (nblk, SUB, HEAD))
            gk = jnp.where(sub3 == tl, k3, gk * fb)
            p = jnp.sum(gk * qb, axis=2, keepdims=True)
            pt = jnp.where(lane3 == blk3 * SUB + tl, p, pt)
        sc = sc + pt.reshape(CHUNK, CHUNK)

        out = _dot(sc.astype(BF16), v, TN) + inter
        ms = jnp.mean(out * out, axis=-1, keepdims=True)
        y = out * lax.rsqrt(ms + EPS) * gain
        o_ref[r, :] = (y * (g * jax.nn.sigmoid(g))).astype(o_ref.dtype)
        return carry

    lax.fori_loop(0, tt // CHUNK, chunk, 0)


def _hgrn(proj, a_lower_bound, a_out_norm, batch, seq, tt=512):
    m = proj.shape[0]
    nt = seq // tt

    def seg(s):
        return pl.BlockSpec((tt, HEAD), lambda b, h, t, s=s: (b * nt + t, s + h))

    return pl.pallas_call(
        functools.partial(_hgrn_kernel, tt=tt),
        grid=(batch, HEADS, nt),
        in_specs=[
            pl.BlockSpec((2, HEAD), lambda b, h, t: (0, h)),
            pl.BlockSpec((1, HEAD), lambda b, h, t: (0, 0)),
            seg(SEG_AQ), seg(SEG_AF), seg(SEG_AI), seg(SEG_AG),
        ],
        out_specs=pl.BlockSpec((tt, HEAD), lambda b, h, t: (b * nt + t, h)),
        out_shape=jax.ShapeDtypeStruct((m, WIDTH), BF16),
        scratch_shapes=[pltpu.VMEM((HEAD, HEAD), F32)],
        compiler_params=pltpu.CompilerParams(
            dimension_semantics=("arbitrary", "arbitrary", "arbitrary"),
            vmem_limit_bytes=VMEM_LIMIT),
        name="hgrn",
    )(a_lower_bound, a_out_norm, proj, proj, proj, proj)


def _qkv_prep_kernel(qg_ref, kg_ref, bq_ref, bk_ref, bv_ref, qn_ref, kn_ref, vt_ref):
    r = lax.broadcasted_iota(jnp.int32, (HEAD, HEAD), 0)
    c = lax.broadcasted_iota(jnp.int32, (HEAD, HEAD), 1)
    grp = ((r >> 6) == (c >> 6)).astype(BF16)

    def gnorm(x, gain):
        x2 = x * x
        hi = x2.astype(BF16)
        lo = (x2 - hi.astype(F32)).astype(BF16)
        ss = _dot(hi, grp, NN) + _dot(lo, grp, NN)
        return x * lax.rsqrt(ss * (1.0 / QK) + EPS) * gain

    qn_ref[...] = (gnorm(bq_ref[...], qg_ref[...]) * (QK ** -0.5)).astype(BF16)
    kn_ref[...] = gnorm(bk_ref[...], kg_ref[...]).astype(BF16)
    vt_ref[0, 0] = bv_ref[...].T.astype(BF16)


def _qkv_prep(proj, q_gain, k_gain, batch, seq, tr=512):
    m = proj.shape[0]
    nt = seq // tr

    def seg(s):
        return pl.BlockSpec((tr, HEAD), lambda b, h, t, s=s: (b * nt + t, s + h))

    row_out = pl.BlockSpec((tr, HEAD), lambda b, h, t: (b * nt + t, h))
    return pl.pallas_call(
        _qkv_prep_kernel,
        grid=(batch, HEADS, nt),
        in_specs=[
            pl.BlockSpec((1, HEAD), lambda b, h, t: (0, 0)),
            pl.BlockSpec((1, HEAD), lambda b, h, t: (0, 0)),
            seg(SEG_BQ), seg(SEG_BK), seg(SEG_BV),
        ],
        out_specs=[row_out, row_out,
                   pl.BlockSpec((1, 1, HEAD, tr), lambda b, h, t: (b, h, 0, t))],
        out_shape=[jax.ShapeDtypeStruct((m, WIDTH), BF16),
                   jax.ShapeDtypeStruct((m, WIDTH), BF16),
                   jax.ShapeDtypeStruct((batch, HEADS, HEAD, seq), BF16)],
        compiler_params=pltpu.CompilerParams(
            dimension_semantics=("arbitrary", "arbitrary", "arbitrary"),
            vmem_limit_bytes=VMEM_LIMIT),
        name="qkv_prep",
    )(q_gain, k_gain, proj, proj, proj)


def _attn_kernel(lam_ref, sub_ref, q_ref, k_ref, vt_ref, g_ref, o_ref, *, tq):
    tk = tq
    h = pl.program_id(1)
    i = pl.program_id(2)

    slope = jnp.exp2(-(jnp.zeros((1, 1), F32) + (h + 1).astype(F32)))
    lv = lam_ref[...]
    lam = (jnp.exp(jnp.sum(lv[0:1, :] * lv[1:2, :], keepdims=True))
           - jnp.exp(jnp.sum(lv[2:3, :] * lv[3:4, :], keepdims=True)) + LAMBDA_INIT)

    q = q_ref[...]
    lane = lax.broadcasted_iota(jnp.int32, (tq, HEAD), 1)
    zero = jnp.zeros_like(q)
    qmaps = (jnp.where(lane < QK, q, zero), jnp.where(lane >= QK, q, zero))

    s_i = lax.broadcasted_iota(jnp.int32, (tk, tq), 0)
    t_i = lax.broadcasted_iota(jnp.int32, (tk, tq), 1)
    s_f = s_i.astype(F32)
    t_f = t_i.astype(F32)
    bias_off = slope * s_f
    bias_diag = slope * jnp.where(s_i <= t_i, s_f, 2.0 * t_f - s_f)
    allowed = (s_i >> 6) <= (t_i >> 6)

    def tile(j, bias, shift, mask, carry):
        kb = k_ref[pl.ds(pl.multiple_of(j * tk, tk), tk), :]
        vtb = vt_ref[0, 0, :, pl.ds(pl.multiple_of(j * tk, tk), tk)]
        new = []
        for mp in range(2):
            m_old, l_old, acc = carry[mp]
            st = _dot(kb, qmaps[mp], NT) + bias
            if mask is not None:
                st = jnp.where(mask, st, NEG)
            m_new = jnp.maximum(m_old, jnp.max(st, axis=0, keepdims=True) + shift)
            p = jnp.exp(st - (m_new - shift))
            alpha = jnp.exp(m_old - m_new)
            l_new = alpha * l_old + jnp.sum(p, axis=0, keepdims=True)
            acc = alpha * acc + _dot(vtb, p.astype(BF16), NN)
            new.append((m_new, l_new, acc))
        return tuple(new)

    init = tuple((jnp.full((1, tq), NEG, F32), jnp.zeros((1, tq), F32),
                  jnp.zeros((HEAD, tq), F32)) for _ in range(2))
    carry = tile(i, bias_diag, jnp.zeros((1, 1), F32), allowed, init)

    def body(j, carry):
        shift = slope * ((j - i) * tk).astype(F32)
        return tile(j, bias_off, shift, None, carry)

    (_, l0, acc0), (_, l1, acc1) = lax.fori_loop(0, i, body, carry)

    ot = acc0 / l0 - lam * (acc1 / l1)
    o = ot.T
    ms = jnp.mean(o * o, axis=-1, keepdims=True)
    y = o * lax.rsqrt(ms + EPS) * sub_ref[...] * (1.0 - LAMBDA_INIT)
    g = g_ref[...]
    o_ref[...] = (y * (g * jax.nn.sigmoid(g))).astype(o_ref.dtype)


def _attn(qn, kn, vt, proj, b_lambda, b_subln, batch, seq, tq=256):
    m = qn.shape[0]
    nq = seq // tq
    return pl.pallas_call(
        functools.partial(_attn_kernel, tq=tq),
        grid=(batch, HEADS, nq),
        in_specs=[
            pl.BlockSpec((4, QK), lambda b, h, i: (0, 0)),
            pl.BlockSpec((1, HEAD), lambda b, h, i: (0, 0)),
            pl.BlockSpec((tq, HEAD), lambda b, h, i: (b * nq + i, h)),
            pl.BlockSpec((seq, HEAD), lambda b, h, i: (b, h)),
            pl.BlockSpec((1, 1, HEAD, seq), lambda b, h, i: (b, h, 0, 0)),
            pl.BlockSpec((tq, HEAD), lambda b, h, i: (b * nq + i, SEG_BG + h)),
        ],
        out_specs=pl.BlockSpec((tq, HEAD), lambda b, h, i: (b * nq + i, h)),
        out_shape=jax.ShapeDtypeStruct((m, WIDTH), BF16),
        compiler_params=pltpu.CompilerParams(
            dimension_semantics=("arbitrary", "arbitrary", "arbitrary"),
            vmem_limit_bytes=VMEM_LIMIT),
        name="attn",
    )(b_lambda, b_subln, qn, kn, vt, proj)


def _out_proj_kernel(ya_ref, yb_ref, ga_ref, gb_ref, x_ref, wa_ref, wb_ref, wo_ref, o_ref):
    ya = jnp.dot(ya_ref[...], wa_ref[...], preferred_element_type=F32)
    yb = jnp.dot(yb_ref[...], wb_ref[...], preferred_element_type=F32)
    mixed = jax.nn.sigmoid(ga_ref[...]) * ya + jax.nn.sigmoid(gb_ref[...]) * yb
    o_ref[...] = x_ref[...] + jnp.dot(mixed.astype(BF16), wo_ref[...],
                                      preferred_element_type=F32)


def _out_proj(ya, yb, proj, x2, wa, wb, wo, tm=256):
    m, d = x2.shape
    w = ya.shape[1]
    resident = functools.partial(pl.BlockSpec, pipeline_mode=pl.Buffered(1))
    return pl.pallas_call(
        _out_proj_kernel,
        grid=(m // tm,),
        in_specs=[
            pl.BlockSpec((tm, w), lambda i: (i, 0)),
            pl.BlockSpec((tm, w), lambda i: (i, 0)),
            pl.BlockSpec((tm, d), lambda i: (i, SEG_GATE_A)),
            pl.BlockSpec((tm, d), lambda i: (i, SEG_GATE_B)),
            pl.BlockSpec((tm, d), lambda i: (i, 0)),
            resident((w, d), lambda i: (0, 0)),
            resident((w, d), lambda i: (0, 0)),
            resident((d, d), lambda i: (0, 0)),
        ],
        out_specs=pl.BlockSpec((tm, d), lambda i: (i, 0)),
        out_shape=jax.ShapeDtypeStruct((m, d), F32),
        compiler_params=pltpu.CompilerParams(
            dimension_semantics=("arbitrary",), vmem_limit_bytes=VMEM_LIMIT),
        name="out_proj",
    )(ya, yb, proj, proj, x2, wa, wb, wo)


def kernel(x, norm_w, w_in, a_lower_bound, a_out_norm, b_q_norm, b_k_norm, b_lambda,
           b_subln, w_branch_a, w_branch_b, w_out):
    batch, seq, d = x.shape
    assert norm_w.shape[0] == 1 and a_lower_bound.shape == (2, WIDTH)
    assert d == 2 * WIDTH and seq % 512 == 0
    x2 = x.reshape(batch * seq, d)

    proj = _in_proj(x2, norm_w[0:1], w_in[0].astype(BF16))
    ya = _hgrn(proj, a_lower_bound, a_out_norm[0:1], batch, seq)
    qn, kn, vt = _qkv_prep(proj, jnp.tile(b_q_norm[0:1], (1, 2)),
                           jnp.tile(b_k_norm[0:1], (1, 2)), batch, seq)
    yb = _attn(qn, kn, vt, proj, b_lambda[0], b_subln[0:1], batch, seq)
    out = _out_proj(ya, yb, proj, x2, w_branch_a[0].astype(BF16),
                    w_branch_b[0].astype(BF16), w_out[0].astype(BF16))
    return out.reshape(batch, seq, d)
```

```python
import functools
import math

import jax
import jax.numpy as jnp
from jax import lax
from jax.experimental import pallas as pl
from jax.experimental.pallas import tpu as pltpu

F32 = jnp.float32
BF16 = jnp.bfloat16

EPS = 1e-6
CHUNK = 64
HEAD = 128
HEADS = 8
QK = 64
WIDTH = HEADS * HEAD
SUB = 8
LAMBDA_INIT = 0.8 - 0.6 * math.exp(-0.3 * 0)
NEG = -1e30

SEG_AQ, SEG_AF, SEG_AI, SEG_AG, SEG_BQ, SEG_BK, SEG_BV, SEG_BG = (s * HEADS for s in range(8))
SEG_GATE_A, SEG_GATE_B = 4, 5

VMEM_LIMIT = 56 * 1024 * 1024


def _dot(a, b, dims):
    return lax.dot_general(a, b, (dims, ((), ())), preferred_element_type=F32)


NN = ((1,), (0,))
NT = ((1,), (1,))
TN = ((0,), (0,))


def _in_proj_kernel(x_ref, nw_ref, w_ref, o_ref, h_ref):
    @pl.when(pl.program_id(1) == 0)
    def _():
        x = x_ref[...]
        ms = jnp.mean(x * x, axis=-1, keepdims=True)
        h_ref[...] = (x * lax.rsqrt(ms + EPS) * nw_ref[...]).astype(BF16)

    o_ref[...] = jnp.dot(h_ref[...], w_ref[...], preferred_element_type=F32)


def _in_proj(x2, norm_w, w_in_bf16, tm=1024, tn=1024):
    m, d = x2.shape
    n = w_in_bf16.shape[1]
    return pl.pallas_call(
        _in_proj_kernel,
        grid=(m // tm, n // tn),
        in_specs=[
            pl.BlockSpec((tm, d), lambda i, j: (i, 0)),
            pl.BlockSpec((1, d), lambda i, j: (0, 0)),
            pl.BlockSpec((d, tn), lambda i, j: (0, j)),
        ],
        out_specs=pl.BlockSpec((tm, tn), lambda i, j: (i, j)),
        out_shape=jax.ShapeDtypeStruct((m, n), F32),
        scratch_shapes=[pltpu.VMEM((tm, d), BF16)],
        compiler_params=pltpu.CompilerParams(
            dimension_semantics=("arbitrary", "arbitrary"), vmem_limit_bytes=VMEM_LIMIT),
        name="in_proj",
    )(x2, norm_w, w_in_bf16)


def _hgrn_kernel(alb_ref, gain_ref, aq_ref, af_ref, ai_ref, ag_ref, o_ref, st_ref, *, tt):
    @pl.when(pl.program_id(2) == 0)
    def _():
        st_ref[...] = jnp.zeros_like(st_ref)

    a = alb_ref[...]
    a0, a1 = a[0:1, :], a[1:2, :]
    amax = jnp.maximum(a0, a1)
    e0, e1 = jnp.exp(a0 - amax), jnp.exp(a1 - amax)
    lb = e0 / (e0 + e1)
    one_m_lb = 1.0 - lb
    gain = gain_ref[...]

    row = lax.broadcasted_iota(jnp.int32, (CHUNK, CHUNK), 0)
    col = lax.broadcasted_iota(jnp.int32, (CHUNK, CHUNK), 1)
    tril = (row >= col).astype(BF16)
    same32 = (row >> 5) == (col >> 5)
    same16 = (row >> 4) == (col >> 4)
    rowc = lax.broadcasted_iota(jnp.int32, (CHUNK, HEAD), 0)
    nblk = CHUNK // SUB
    sub3 = lax.broadcasted_iota(jnp.int32, (nblk, SUB, HEAD), 1)
    lane3 = lax.broadcasted_iota(jnp.int32, (nblk, SUB, CHUNK), 2)
    blk3 = lax.broadcasted_iota(jnp.int32, (nblk, SUB, CHUNK), 0)

    def level(b, q, k, blk):
        half = blk // 2
        mids = [jnp.broadcast_to(b[j * blk + half:j * blk + half + 1, :], (blk, HEAD))
                for j in range(CHUNK // blk)]
        mid = mids[0] if len(mids) == 1 else jnp.concatenate(mids, axis=0)
        upper = (rowc & (blk - 1)) >= half
        d = b - mid
        e = jnp.exp(jnp.where(upper, d, -d))
        ql = jnp.where(upper, q * e, 0.0).astype(BF16)
        kl = jnp.where(upper, 0.0, k * e).astype(BF16)
        return _dot(kl, ql, NT)

    def chunk(c, carry):
        r = pl.ds(pl.multiple_of(c * CHUNK, CHUNK), CHUNK)
        z = af_ref[r, :]
        qa = aq_ref[r, :]
        v = ai_ref[r, :].astype(BF16)
        g = ag_ref[r, :]

        sig = jax.nn.sigmoid(z)
        f = lb + one_m_lb * sig
        lf = jnp.log(f)
        k = one_m_lb * (1.0 - sig)
        q = qa * jax.nn.sigmoid(qa)

        hi = lf.astype(BF16)
        r1 = lf - hi.astype(F32)
        mid_ = r1.astype(BF16)
        lo = (r1 - mid_.astype(F32)).astype(BF16)
        b = _dot(tril, hi, NN) + _dot(tril, mid_, NN) + _dot(tril, lo, NN)
        b_last = b[CHUNK - 1:CHUNK, :]

        st = st_ref[...]
        inter = _dot((q * jnp.exp(b)).astype(BF16), st.astype(BF16), NT)
        kd = (k * jnp.exp(b_last - b)).astype(BF16)
        st_ref[...] = st * jnp.exp(b_last) + _dot(v, kd, TN)

        sc = level(b, q, k, 64)
        sc = sc + jnp.where(same32, level(b, q, k, 32), 0.0)
        sc = sc + jnp.where(same16, level(b, q, k, 16), 0.0)

        k3 = k.reshape(nblk, SUB, HEAD)
        q3 = q.reshape(nblk, SUB, HEAD)
        f3 = f.reshape(nblk, SUB, HEAD)
        gk = jnp.zeros((nblk, SUB, HEAD), F32)
        pt = jnp.zeros((nblk, SUB, CHUNK), F32)
        for tl in range(SUB):
            fb = jnp.broadcast_to(f3[:, tl:tl + 1, :], (nblk, SUB, HEAD))
            qb = jnp.broadcast_to(q3[:, tl:tl + 1, :], (nblk, SUB, HEAD))
            gk = jnp.where(sub3 == tl, k3, gk * fb)
            p = jnp.sum(gk * qb, axis=2, keepdims=True)
            pt = jnp.where(lane3 == blk3 * SUB + tl, p, pt)
        sc = sc + pt.reshape(CHUNK, CHUNK)

        out = _dot(sc.astype(BF16), v, TN) + inter
        ms = jnp.mean(out * out, axis=-1, keepdims=True)
        y = out * lax.rsqrt(ms + EPS) * gain
        o_ref[r, :] = (y * (g * jax.nn.sigmoid(g))).astype(o_ref.dtype)
        return carry

    lax.fori_loop(0, tt // CHUNK, chunk, 0)


def _hgrn(proj, a_lower_bound, a_out_norm, batch, seq, tt=512):
    m = proj.shape[0]
    nt = seq // tt

    def seg(s):
        return pl.BlockSpec((tt, HEAD), lambda b, h, t, s=s: (b * nt + t, s + h))

    return pl.pallas_call(
        functools.partial(_hgrn_kernel, tt=tt),
        grid=(batch, HEADS, nt),
        in_specs=[
            pl.BlockSpec((2, HEAD), lambda b, h, t: (0, h)),
            pl.BlockSpec((1, HEAD), lambda b, h, t: (0, 0)),
            seg(SEG_AQ), seg(SEG_AF), seg(SEG_AI), seg(SEG_AG),
        ],
        out_specs=pl.BlockSpec((tt, HEAD), lambda b, h, t: (b * nt + t, h)),
        out_shape=jax.ShapeDtypeStruct((m, WIDTH), BF16),
        scratch_shapes=[pltpu.VMEM((HEAD, HEAD), F32)],
        compiler_params=pltpu.CompilerParams(
            dimension_semantics=("arbitrary", "arbitrary", "arbitrary"),
            vmem_limit_bytes=VMEM_LIMIT),
        name="hgrn",
    )(a_lower_bound, a_out_norm, proj, proj, proj, proj)


def _qkv_prep_kernel(qg_ref, kg_ref, bq_ref, bk_ref, bv_ref, qt_ref, kn_ref, vt_ref):
    r = lax.broadcasted_iota(jnp.int32, (HEAD, HEAD), 0)
    c = lax.broadcasted_iota(jnp.int32, (HEAD, HEAD), 1)
    grp = ((r >> 6) == (c >> 6)).astype(BF16)

    def gnorm(x, gain):
        x2 = x * x
        hi = x2.astype(BF16)
        lo = (x2 - hi.astype(F32)).astype(BF16)
        ss = _dot(hi, grp, NN) + _dot(lo, grp, NN)
        return x * lax.rsqrt(ss * (1.0 / QK) + EPS) * gain

    qt_ref[0, 0] = (gnorm(bq_ref[...], qg_ref[...]) * (QK ** -0.5)).T.astype(BF16)
    kn_ref[...] = gnorm(bk_ref[...], kg_ref[...]).astype(BF16)
    vt_ref[0, 0] = bv_ref[...].T.astype(BF16)


def _qkv_prep(proj, q_gain, k_gain, batch, seq, tr=512):
    m = proj.shape[0]
    nt = seq // tr

    def seg(s):
        return pl.BlockSpec((tr, HEAD), lambda b, h, t, s=s: (b * nt + t, s + h))

    row_out = pl.BlockSpec((tr, HEAD), lambda b, h, t: (b * nt + t, h))
    col_out = pl.BlockSpec((1, 1, HEAD, tr), lambda b, h, t: (b, h, 0, t))
    col_shape = jax.ShapeDtypeStruct((batch, HEADS, HEAD, seq), BF16)
    return pl.pallas_call(
        _qkv_prep_kernel,
        grid=(batch, HEADS, nt),
        in_specs=[
            pl.BlockSpec((1, HEAD), lambda b, h, t: (0, 0)),
            pl.BlockSpec((1, HEAD), lambda b, h, t: (0, 0)),
            seg(SEG_BQ), seg(SEG_BK), seg(SEG_BV),
        ],
        out_specs=[col_out, row_out, col_out],
        out_shape=[col_shape, jax.ShapeDtypeStruct((m, WIDTH), BF16), col_shape],
        compiler_params=pltpu.CompilerParams(
            dimension_semantics=("arbitrary", "arbitrary", "arbitrary"),
            vmem_limit_bytes=VMEM_LIMIT),
        name="qkv_prep",
    )(q_gain, k_gain, proj, proj, proj)


def _attn_kernel(lam_ref, sub_ref, qt_ref, k_ref, vt_ref, g_ref, o_ref,
                 sa0, sa1, pb0, pb1, acc_ref, addb_ref, *, tq, tk):
    sa = (sa0, sa1)
    pb = (pb0, pb1)
    h = pl.program_id(1)
    i = pl.program_id(2)
    slope = jnp.exp2(-(jnp.zeros((1, 1), F32) + (h + 1).astype(F32)))

    @pl.when(i == 0)
    def _():
        t_i = lax.broadcasted_iota(jnp.int32, (tk, tq), 1)
        for d in range(2):
            s_i = lax.broadcasted_iota(jnp.int32, (tk, tq), 0) + d * tk
            corr = slope * jnp.where(s_i <= t_i, 0, 2 * (t_i - s_i)).astype(F32)
            addb_ref[d] = jnp.where((s_i >> 6) <= (t_i >> 6), corr, NEG)

    qt = qt_ref[0, 0]
    rowd = lax.broadcasted_iota(jnp.int32, (HEAD, tq), 0)
    auxq = jnp.where(rowd == 0, slope, 0.0).astype(BF16)
    zero = jnp.zeros_like(qt)
    qx = (jnp.concatenate([jnp.where(rowd < QK, qt, zero), auxq], axis=0),
          jnp.concatenate([jnp.where(rowd >= QK, qt, zero), auxq], axis=0))
    lane_k = lax.broadcasted_iota(jnp.int32, (tk, HEAD), 1)
    row_k = lax.broadcasted_iota(jnp.int32, (tk, HEAD), 0)
    auxk = jnp.where(lane_k == 0, row_k, 0).astype(F32).astype(BF16)

    acc_ref[...] = jnp.zeros_like(acc_ref)
    pb1[...] = jnp.zeros_like(pb1)

    def stage_a(j, slot):
        kb = k_ref[pl.ds(pl.multiple_of(j * tk, tk), tk), :]
        kx = jnp.concatenate([kb, auxk], axis=1)
        for mp in range(2):
            sa[slot][mp] = _dot(kx, qx[mp], NN)

    def stage_b(slot, shift, addb, ml):
        new_ml, alphas = [], []
        for mp in range(2):
            m_old, l_old = ml[mp]
            st = sa[slot][mp]
            if addb is not None:
                st = st + addb
            m_new = jnp.maximum(m_old, jnp.max(st, axis=0, keepdims=True) + shift)
            p = jnp.exp(st - (m_new - shift))
            alpha = jnp.exp(m_old - m_new)
            pb[slot][mp] = p.astype(BF16)
            new_ml.append((m_new, alpha * l_old + jnp.sum(p, axis=0, keepdims=True)))
            alphas.append(alpha)
        return tuple(new_ml), tuple(alphas)

    def stage_c(j, slot, alphas):
        vtb = vt_ref[0, 0, :, pl.ds(pl.multiple_of(j * tk, tk), tk)]
        for mp in range(2):
            acc_ref[mp] = alphas[mp] * acc_ref[mp] + _dot(vtb, pb[slot][mp], NN)

    def off_shift(j):
        return slope * (j * tk - i * tq).astype(F32)

    ml = tuple((jnp.full((1, tq), NEG, F32), jnp.zeros((1, tq), F32)) for _ in range(2))
    ones = tuple(jnp.ones((1, tq), F32) for _ in range(2))

    stage_a(0, 0)

    def pair(p, carry):
        ml, al_prev = carry
        j0 = 2 * p
        stage_a(j0 + 1, 1)
        ml, al0 = stage_b(0, off_shift(j0), None, ml)
        stage_c(jnp.maximum(j0 - 1, 0), 1, al_prev)
        stage_a(j0 + 2, 0)
        ml, al1 = stage_b(1, off_shift(j0 + 1), None, ml)
        stage_c(j0, 0, al0)
        return ml, al1

    ml, al_prev = lax.fori_loop(0, i, pair, (ml, ones))
    jd = 2 * i
    stage_a(jd + 1, 1)
    ml, al0 = stage_b(0, jnp.zeros((1, 1), F32), addb_ref[0], ml)
    stage_c(jnp.maximum(jd - 1, 0), 1, al_prev)
    ml, al1 = stage_b(1, slope * tk, addb_ref[1], ml)
    stage_c(jd, 0, al0)
    stage_c(jd + 1, 1, al1)

    lv = lam_ref[...]
    lam = (jnp.exp(jnp.sum(lv[0:1, :] * lv[1:2, :], keepdims=True))
           - jnp.exp(jnp.sum(lv[2:3, :] * lv[3:4, :], keepdims=True)) + LAMBDA_INIT)
    (_, l0), (_, l1) = ml
    ot = acc_ref[0] / l0 - lam * (acc_ref[1] / l1)
    o = ot.T
    ms = jnp.mean(o * o, axis=-1, keepdims=True)
    y = o * lax.rsqrt(ms + EPS) * sub_ref[...] * (1.0 - LAMBDA_INIT)
    g = g_ref[...]
    o_ref[...] = (y * (g * jax.nn.sigmoid(g))).astype(o_ref.dtype)


def _attn(qt, kn, vt, proj, b_lambda, b_subln, batch, seq, tq=512, tk=256):
    assert tq == 2 * tk and seq % tq == 0
    m = kn.shape[0]
    nq = seq // tq
    return pl.pallas_call(
        functools.partial(_attn_kernel, tq=tq, tk=tk),
        grid=(batch, HEADS, nq),
        in_specs=[
            pl.BlockSpec((4, QK), lambda b, h, i: (0, 0)),
            pl.BlockSpec((1, HEAD), lambda b, h, i: (0, 0)),
            pl.BlockSpec((1, 1, HEAD, tq), lambda b, h, i: (b, h, 0, i)),
            pl.BlockSpec((seq, HEAD), lambda b, h, i: (b, h)),
            pl.BlockSpec((1, 1, HEAD, seq), lambda b, h, i: (b, h, 0, 0)),
            pl.BlockSpec((tq, HEAD), lambda b, h, i: (b * nq + i, SEG_BG + h)),
        ],
        out_specs=pl.BlockSpec((tq, HEAD), lambda b, h, i: (b * nq + i, h)),
        out_shape=jax.ShapeDtypeStruct((m, WIDTH), BF16),
        scratch_shapes=[
            pltpu.VMEM((2, tk, tq), F32), pltpu.VMEM((2, tk, tq), F32),
            pltpu.VMEM((2, tk, tq), BF16), pltpu.VMEM((2, tk, tq), BF16),
            pltpu.VMEM((2, HEAD, tq), F32), pltpu.VMEM((2, tk, tq), F32),
        ],
        compiler_params=pltpu.CompilerParams(
            dimension_semantics=("arbitrary", "arbitrary", "arbitrary"),
            vmem_limit_bytes=VMEM_LIMIT),
        name="attn",
    )(b_lambda, b_subln, qt, kn, vt, proj)


def _out_proj_kernel(ya_ref, yb_ref, ga_ref, gb_ref, x_ref, wa_ref, wb_ref, wo_ref, o_ref):
    ya = jnp.dot(ya_ref[...], wa_ref[...], preferred_element_type=F32)
    yb = jnp.dot(yb_ref[...], wb_ref[...], preferred_element_type=F32)
    mixed = jax.nn.sigmoid(ga_ref[...]) * ya + jax.nn.sigmoid(gb_ref[...]) * yb
    o_ref[...] = x_ref[...] + jnp.dot(mixed.astype(BF16), wo_ref[...],
                                      preferred_element_type=F32)


def _out_proj(ya, yb, proj, x2, wa, wb, wo, tm=256):
    m, d = x2.shape
    w = ya.shape[1]
    resident = functools.partial(pl.BlockSpec, pipeline_mode=pl.Buffered(1))
    return pl.pallas_call(
        _out_proj_kernel,
        grid=(m // tm,),
        in_specs=[
            pl.BlockSpec((tm, w), lambda i: (i, 0)),
            pl.BlockSpec((tm, w), lambda i: (i, 0)),
            pl.BlockSpec((tm, d), lambda i: (i, SEG_GATE_A)),
            pl.BlockSpec((tm, d), lambda i: (i, SEG_GATE_B)),
            pl.BlockSpec((tm, d), lambda i: (i, 0)),
            resident((w, d), lambda i: (0, 0)),
            resident((w, d), lambda i: (0, 0)),
            resident((d, d), lambda i: (0, 0)),
        ],
        out_specs=pl.BlockSpec((tm, d), lambda i: (i, 0)),
        out_shape=jax.ShapeDtypeStruct((m, d), F32),
        compiler_params=pltpu.CompilerParams(
            dimension_semantics=("arbitrary",), vmem_limit_bytes=VMEM_LIMIT),
        name="out_proj",
    )(ya, yb, proj, proj, x2, wa, wb, wo)


def kernel(x, norm_w, w_in, a_lower_bound, a_out_norm, b_q_norm, b_k_norm, b_lambda,
           b_subln, w_branch_a, w_branch_b, w_out):
    batch, seq, d = x.shape
    assert norm_w.shape[0] == 1 and a_lower_bound.shape == (2, WIDTH)
    assert d == 2 * WIDTH and seq % 512 == 0
    x2 = x.reshape(batch * seq, d)

    proj = _in_proj(x2, norm_w[0:1], w_in[0].astype(BF16))
    ya = _hgrn(proj, a_lower_bound, a_out_norm[0:1], batch, seq)
    qt, kn, vt = _qkv_prep(proj, jnp.tile(b_q_norm[0:1], (1, 2)),
                           jnp.tile(b_k_norm[0:1], (1, 2)), batch, seq)
    yb = _attn(qt, kn, vt, proj, b_lambda[0], b_subln[0:1], batch, seq)
    out = _out_proj(ya, yb, proj, x2, w_branch_a[0].astype(BF16),
                    w_branch_b[0].astype(BF16), w_out[0].astype(BF16))
    return out.reshape(batch, seq, d)
```

```python
import functools
import math

import jax
import jax.numpy as jnp
from jax import lax
from jax.experimental import pallas as pl
from jax.experimental.pallas import tpu as pltpu

F32 = jnp.float32
BF16 = jnp.bfloat16

EPS = 1e-6
CHUNK = 64
HEAD = 128
HEADS = 8
QK = 64
WIDTH = HEADS * HEAD
SUB = 8
LAMBDA_INIT = 0.8 - 0.6 * math.exp(-0.3 * 0)
NEG = -1e30

SEG_AQ, SEG_AF, SEG_AI, SEG_AG, SEG_BQ, SEG_BK, SEG_BV, SEG_BG = (s * HEADS for s in range(8))
SEG_GATE_A, SEG_GATE_B = 4, 5

VMEM_LIMIT = 56 * 1024 * 1024


def _dot(a, b, dims):
    return lax.dot_general(a, b, (dims, ((), ())), preferred_element_type=F32)


NN = ((1,), (0,))
NT = ((1,), (1,))
TN = ((0,), (0,))


def _in_proj_kernel(x_ref, nw_ref, w_ref, o_ref, h_ref):
    @pl.when(pl.program_id(1) == 0)
    def _():
        x = x_ref[...]
        ms = jnp.mean(x * x, axis=-1, keepdims=True)
        h_ref[...] = (x * lax.rsqrt(ms + EPS) * nw_ref[...]).astype(BF16)

    o_ref[...] = jnp.dot(h_ref[...], w_ref[...], preferred_element_type=F32)


def _in_proj(x2, norm_w, w_in_bf16, tm=1024, tn=1024):
    m, d = x2.shape
    n = w_in_bf16.shape[1]
    return pl.pallas_call(
        _in_proj_kernel,
        grid=(m // tm, n // tn),
        in_specs=[
            pl.BlockSpec((tm, d), lambda i, j: (i, 0)),
            pl.BlockSpec((1, d), lambda i, j: (0, 0)),
            pl.BlockSpec((d, tn), lambda i, j: (0, j)),
        ],
        out_specs=pl.BlockSpec((tm, tn), lambda i, j: (i, j)),
        out_shape=jax.ShapeDtypeStruct((m, n), F32),
        scratch_shapes=[pltpu.VMEM((tm, d), BF16)],
        compiler_params=pltpu.CompilerParams(
            dimension_semantics=("arbitrary", "arbitrary"), vmem_limit_bytes=VMEM_LIMIT),
        name="in_proj",
    )(x2, norm_w, w_in_bf16)


def _hgrn_kernel(alb_ref, gain_ref, aq_ref, af_ref, ai_ref, ag_ref, o_ref, st_ref, *, tt):
    @pl.when(pl.program_id(2) == 0)
    def _():
        st_ref[...] = jnp.zeros_like(st_ref)

    a = alb_ref[...]
    a0, a1 = a[0:1, :], a[1:2, :]
    amax = jnp.maximum(a0, a1)
    e0, e1 = jnp.exp(a0 - amax), jnp.exp(a1 - amax)
    lb = e0 / (e0 + e1)
    one_m_lb = 1.0 - lb
    gain = gain_ref[...]

    nch = tt // CHUNK
    row = lax.broadcasted_iota(jnp.int32, (CHUNK, CHUNK), 0)
    col = lax.broadcasted_iota(jnp.int32, (CHUNK, CHUNK), 1)
    tril = (row >= col).astype(BF16)
    same32 = (row >> 5) == (col >> 5)
    same16 = (row >> 4) == (col >> 4)
    rowt = lax.broadcasted_iota(jnp.int32, (tt, HEAD), 0)
    nblk = tt // SUB
    sub3 = lax.broadcasted_iota(jnp.int32, (nblk, SUB, HEAD), 1)
    lane3 = lax.broadcasted_iota(jnp.int32, (nblk, SUB, CHUNK), 2)
    blk3 = lax.broadcasted_iota(jnp.int32, (nblk, SUB, CHUNK), 0)

    def rows(x, c):
        return x[c * CHUNK:(c + 1) * CHUNK]

    def row_bcast(x, step, off):
        return jnp.concatenate(
            [jnp.broadcast_to(x[j * step + off:j * step + off + 1, :], (step, HEAD))
             for j in range(tt // step)], axis=0)

    z = af_ref[...]
    qa = aq_ref[...]
    v = ai_ref[...].astype(BF16)
    g = ag_ref[...]
    sig = jax.nn.sigmoid(z)
    f = lb + one_m_lb * sig
    lf = jnp.log(f)
    k = one_m_lb * (1.0 - sig)
    q = qa * jax.nn.sigmoid(qa)

    hi = lf.astype(BF16)
    r1 = lf - hi.astype(F32)
    md = r1.astype(BF16)
    lo = (r1 - md.astype(F32)).astype(BF16)
    b = jnp.concatenate(
        [_dot(tril, rows(hi, c), NN) + _dot(tril, rows(md, c), NN) + _dot(tril, rows(lo, c), NN)
         for c in range(nch)], axis=0)
    b_last = row_bcast(b, CHUNK, CHUNK - 1)
    qe = (q * jnp.exp(b)).astype(BF16)
    kd = (k * jnp.exp(b_last - b)).astype(BF16)
    dec = jnp.exp(b_last)

    def level(blk):
        half = blk // 2
        upper = (rowt & (blk - 1)) >= half
        d = b - row_bcast(b, blk, half)
        e = jnp.exp(jnp.where(upper, d, -d))
        ql = jnp.where(upper, q * e, 0.0).astype(BF16)
        kl = jnp.where(upper, 0.0, k * e).astype(BF16)
        return [_dot(rows(kl, c), rows(ql, c), NT) for c in range(nch)]

    lv64, lv32, lv16 = level(64), level(32), level(16)

    k3 = k.reshape(nblk, SUB, HEAD)
    q3 = q.reshape(nblk, SUB, HEAD)
    f3 = f.reshape(nblk, SUB, HEAD)
    gk = jnp.zeros((nblk, SUB, HEAD), F32)
    pt = jnp.zeros((nblk, SUB, CHUNK), F32)
    tcol = (blk3 & (CHUNK // SUB - 1)) * SUB
    for tl in range(SUB):
        fb = jnp.broadcast_to(f3[:, tl:tl + 1, :], (nblk, SUB, HEAD))
        qb = jnp.broadcast_to(q3[:, tl:tl + 1, :], (nblk, SUB, HEAD))
        gk = jnp.where(sub3 == tl, k3, gk * fb)
        p = jnp.sum(gk * qb, axis=2, keepdims=True)
        pt = jnp.where(lane3 == tcol + tl, p, pt)
    pt = pt.reshape(tt, CHUNK)

    st = st_ref[...]
    outs = []
    for c in range(nch):
        sc = (lv64[c] + jnp.where(same32, lv32[c], 0.0) + jnp.where(same16, lv16[c], 0.0)
              + rows(pt, c))
        intra = _dot(sc.astype(BF16), rows(v, c), TN)
        inter = _dot(rows(qe, c), st.astype(BF16), NT)
        st = st * dec[c * CHUNK:c * CHUNK + 1] + _dot(rows(v, c), rows(kd, c), TN)
        outs.append(intra + inter)
    st_ref[...] = st

    out = jnp.concatenate(outs, axis=0)
    ms = jnp.mean(out * out, axis=-1, keepdims=True)
    y = out * lax.rsqrt(ms + EPS) * gain
    o_ref[...] = (y * (g * jax.nn.sigmoid(g))).astype(o_ref.dtype)


def _hgrn(proj, a_lower_bound, a_out_norm, batch, seq, tt=512):
    m = proj.shape[0]
    nt = seq // tt

    def seg(s):
        return pl.BlockSpec((tt, HEAD), lambda b, h, t, s=s: (b * nt + t, s + h))

    return pl.pallas_call(
        functools.partial(_hgrn_kernel, tt=tt),
        grid=(batch, HEADS, nt),
        in_specs=[
            pl.BlockSpec((2, HEAD), lambda b, h, t: (0, h)),
            pl.BlockSpec((1, HEAD), lambda b, h, t: (0, 0)),
            seg(SEG_AQ), seg(SEG_AF), seg(SEG_AI), seg(SEG_AG),
        ],
        out_specs=pl.BlockSpec((tt, HEAD), lambda b, h, t: (b * nt + t, h)),
        out_shape=jax.ShapeDtypeStruct((m, WIDTH), BF16),
        scratch_shapes=[pltpu.VMEM((HEAD, HEAD), F32)],
        compiler_params=pltpu.CompilerParams(
            dimension_semantics=("arbitrary", "arbitrary", "arbitrary"),
            vmem_limit_bytes=VMEM_LIMIT),
        name="hgrn",
    )(a_lower_bound, a_out_norm, proj, proj, proj, proj)


def _qkv_prep_kernel(qg_ref, kg_ref, bq_ref, bk_ref, bv_ref, qt_ref, kn_ref, vt_ref):
    r = lax.broadcasted_iota(jnp.int32, (HEAD, HEAD), 0)
    c = lax.broadcasted_iota(jnp.int32, (HEAD, HEAD), 1)
    grp = ((r >> 6) == (c >> 6)).astype(BF16)

    def gnorm(x, gain):
        x2 = x * x
        hi = x2.astype(BF16)
        lo = (x2 - hi.astype(F32)).astype(BF16)
        ss = _dot(hi, grp, NN) + _dot(lo, grp, NN)
        return x * lax.rsqrt(ss * (1.0 / QK) + EPS) * gain

    qt_ref[0, 0] = (gnorm(bq_ref[...], qg_ref[...]) * (QK ** -0.5)).T.astype(BF16)
    kn_ref[...] = gnorm(bk_ref[...], kg_ref[...]).astype(BF16)
    vt_ref[0, 0] = bv_ref[...].T.astype(BF16)


def _qkv_prep(proj, q_gain, k_gain, batch, seq, tr=512):
    m = proj.shape[0]
    nt = seq // tr

    def seg(s):
        return pl.BlockSpec((tr, HEAD), lambda b, h, t, s=s: (b * nt + t, s + h))

    row_out = pl.BlockSpec((tr, HEAD), lambda b, h, t: (b * nt + t, h))
    col_out = pl.BlockSpec((1, 1, HEAD, tr), lambda b, h, t: (b, h, 0, t))
    col_shape = jax.ShapeDtypeStruct((batch, HEADS, HEAD, seq), BF16)
    return pl.pallas_call(
        _qkv_prep_kernel,
        grid=(batch, HEADS, nt),
        in_specs=[
            pl.BlockSpec((1, HEAD), lambda b, h, t: (0, 0)),
            pl.BlockSpec((1, HEAD), lambda b, h, t: (0, 0)),
            seg(SEG_BQ), seg(SEG_BK), seg(SEG_BV),
        ],
        out_specs=[col_out, row_out, col_out],
        out_shape=[col_shape, jax.ShapeDtypeStruct((m, WIDTH), BF16), col_shape],
        compiler_params=pltpu.CompilerParams(
            dimension_semantics=("arbitrary", "arbitrary", "arbitrary"),
            vmem_limit_bytes=VMEM_LIMIT),
        name="qkv_prep",
    )(q_gain, k_gain, proj, proj, proj)


def _attn_kernel(lam_ref, sub_ref, qt_ref, k_ref, vt_ref, g_ref, o_ref,
                 sa0, sa1, pb0, pb1, acc_ref, addb_ref, *, tq, tk):
    sa = (sa0, sa1)
    pb = (pb0, pb1)
    h = pl.program_id(1)
    i = pl.program_id(2)
    slope = jnp.exp2(-(jnp.zeros((1, 1), F32) + (h + 1).astype(F32)))

    @pl.when(i == 0)
    def _():
        t_i = lax.broadcasted_iota(jnp.int32, (tk, tq), 1)
        for d in range(2):
            s_i = lax.broadcasted_iota(jnp.int32, (tk, tq), 0) + d * tk
            corr = slope * jnp.where(s_i <= t_i, 0, 2 * (t_i - s_i)).astype(F32)
            addb_ref[d] = jnp.where((s_i >> 6) <= (t_i >> 6), corr, NEG)

    qt = qt_ref[0, 0]
    rowd = lax.broadcasted_iota(jnp.int32, (HEAD, tq), 0)
    auxq = jnp.where(rowd == 0, slope, 0.0).astype(BF16)
    zero = jnp.zeros_like(qt)
    qx = (jnp.concatenate([jnp.where(rowd < QK, qt, zero), auxq], axis=0),
          jnp.concatenate([jnp.where(rowd >= QK, qt, zero), auxq], axis=0))
    lane_k = lax.broadcasted_iota(jnp.int32, (tk, HEAD), 1)
    row_k = lax.broadcasted_iota(jnp.int32, (tk, HEAD), 0)
    auxk = jnp.where(lane_k == 0, row_k, 0).astype(F32).astype(BF16)

    acc_ref[...] = jnp.zeros_like(acc_ref)
    pb1[...] = jnp.zeros_like(pb1)

    def stage_a(j, slot):
        kb = k_ref[pl.ds(pl.multiple_of(j * tk, tk), tk), :]
        kx = jnp.concatenate([kb, auxk], axis=1)
        for mp in range(2):
            sa[slot][mp] = _dot(kx, qx[mp], NN)

    def stage_b(slot, shift, addb, ml):
        new_ml, alphas = [], []
        for mp in range(2):
            m_old, l_old = ml[mp]
            st = sa[slot][mp]
            if addb is not None:
                st = st + addb
            m_new = jnp.maximum(m_old, jnp.max(st, axis=0, keepdims=True) + shift)
            p = jnp.exp(st - (m_new - shift))
            alpha = jnp.exp(m_old - m_new)
            pb[slot][mp] = p.astype(BF16)
            new_ml.append((m_new, alpha * l_old + jnp.sum(p, axis=0, keepdims=True)))
            alphas.append(alpha)
        return tuple(new_ml), tuple(alphas)

    def stage_c(j, slot, alphas):
        vtb = vt_ref[0, 0, :, pl.ds(pl.multiple_of(j * tk, tk), tk)]
        for mp in range(2):
            acc_ref[mp] = alphas[mp] * acc_ref[mp] + _dot(vtb, pb[slot][mp], NN)

    def off_shift(j):
        return slope * (j * tk - i * tq).astype(F32)

    ml = tuple((jnp.full((1, tq), NEG, F32), jnp.zeros((1, tq), F32)) for _ in range(2))
    ones = tuple(jnp.ones((1, tq), F32) for _ in range(2))

    stage_a(0, 0)

    def pair(p, carry):
        ml, al_prev = carry
        j0 = 2 * p
        stage_a(j0 + 1, 1)
        ml, al0 = stage_b(0, off_shift(j0), None, ml)
        stage_c(jnp.maximum(j0 - 1, 0), 1, al_prev)
        stage_a(j0 + 2, 0)
        ml, al1 = stage_b(1, off_shift(j0 + 1), None, ml)
        stage_c(j0, 0, al0)
        return ml, al1

    ml, al_prev = lax.fori_loop(0, i, pair, (ml, ones))
    jd = 2 * i
    stage_a(jd + 1, 1)
    ml, al0 = stage_b(0, jnp.zeros((1, 1), F32), addb_ref[0], ml)
    stage_c(jnp.maximum(jd - 1, 0), 1, al_prev)
    ml, al1 = stage_b(1, slope * tk, addb_ref[1], ml)
    stage_c(jd, 0, al0)
    stage_c(jd + 1, 1, al1)

    lv = lam_ref[...]
    lam = (jnp.exp(jnp.sum(lv[0:1, :] * lv[1:2, :], keepdims=True))
           - jnp.exp(jnp.sum(lv[2:3, :] * lv[3:4, :], keepdims=True)) + LAMBDA_INIT)
    (_, l0), (_, l1) = ml
    ot = acc_ref[0] / l0 - lam * (acc_ref[1] / l1)
    o = ot.T
    ms = jnp.mean(o * o, axis=-1, keepdims=True)
    y = o * lax.rsqrt(ms + EPS) * sub_ref[...] * (1.0 - LAMBDA_INIT)
    g = g_ref[...]
    o_ref[...] = (y * (g * jax.nn.sigmoid(g))).astype(o_ref.dtype)


def _attn(qt, kn, vt, proj, b_lambda, b_subln, batch, seq, tq=512, tk=256):
    assert tq == 2 * tk and seq % tq == 0
    m = kn.shape[0]
    nq = seq // tq
    return pl.pallas_call(
        functools.partial(_attn_kernel, tq=tq, tk=tk),
        grid=(batch, HEADS, nq),
        in_specs=[
            pl.BlockSpec((4, QK), lambda b, h, i: (0, 0)),
            pl.BlockSpec((1, HEAD), lambda b, h, i: (0, 0)),
            pl.BlockSpec((1, 1, HEAD, tq), lambda b, h, i: (b, h, 0, i)),
            pl.BlockSpec((seq, HEAD), lambda b, h, i: (b, h)),
            pl.BlockSpec((1, 1, HEAD, seq), lambda b, h, i: (b, h, 0, 0)),
            pl.BlockSpec((tq, HEAD), lambda b, h, i: (b * nq + i, SEG_BG + h)),
        ],
        out_specs=pl.BlockSpec((tq, HEAD), lambda b, h, i: (b * nq + i, h)),
        out_shape=jax.ShapeDtypeStruct((m, WIDTH), BF16),
        scratch_shapes=[
            pltpu.VMEM((2, tk, tq), F32), pltpu.VMEM((2, tk, tq), F32),
            pltpu.VMEM((2, tk, tq), BF16), pltpu.VMEM((2, tk, tq), BF16),
            pltpu.VMEM((2, HEAD, tq), F32), pltpu.VMEM((2, tk, tq), F32),
        ],
        compiler_params=pltpu.CompilerParams(
            dimension_semantics=("arbitrary", "arbitrary", "arbitrary"),
            vmem_limit_bytes=VMEM_LIMIT),
        name="attn",
    )(b_lambda, b_subln, qt, kn, vt, proj)


def _out_proj_kernel(ya_ref, yb_ref, ga_ref, gb_ref, x_ref, wa_ref, wb_ref, wo_ref, o_ref):
    ya = jnp.dot(ya_ref[...], wa_ref[...], preferred_element_type=F32)
    yb = jnp.dot(yb_ref[...], wb_ref[...], preferred_element_type=F32)
    mixed = jax.nn.sigmoid(ga_ref[...]) * ya + jax.nn.sigmoid(gb_ref[...]) * yb
    o_ref[...] = x_ref[...] + jnp.dot(mixed.astype(BF16), wo_ref[...],
                                      preferred_element_type=F32)


def _out_proj(ya, yb, proj, x2, wa, wb, wo, tm=256):
    m, d = x2.shape
    w = ya.shape[1]
    resident = functools.partial(pl.BlockSpec, pipeline_mode=pl.Buffered(1))
    return pl.pallas_call(
        _out_proj_kernel,
        grid=(m // tm,),
        in_specs=[
            pl.BlockSpec((tm, w), lambda i: (i, 0)),
            pl.BlockSpec((tm, w), lambda i: (i, 0)),
            pl.BlockSpec((tm, d), lambda i: (i, SEG_GATE_A)),
            pl.BlockSpec((tm, d), lambda i: (i, SEG_GATE_B)),
            pl.BlockSpec((tm, d), lambda i: (i, 0)),
            resident((w, d), lambda i: (0, 0)),
            resident((w, d), lambda i: (0, 0)),
            resident((d, d), lambda i: (0, 0)),
        ],
        out_specs=pl.BlockSpec((tm, d), lambda i: (i, 0)),
        out_shape=jax.ShapeDtypeStruct((m, d), F32),
        compiler_params=pltpu.CompilerParams(
            dimension_semantics=("arbitrary",), vmem_limit_bytes=VMEM_LIMIT),
        name="out_proj",
    )(ya, yb, proj, proj, x2, wa, wb, wo)


def kernel(x, norm_w, w_in, a_lower_bound, a_out_norm, b_q_norm, b_k_norm, b_lambda,
           b_subln, w_branch_a, w_branch_b, w_out):
    batch, seq, d = x.shape
    assert norm_w.shape[0] == 1 and a_lower_bound.shape == (2, WIDTH)
    assert d == 2 * WIDTH and seq % 512 == 0
    x2 = x.reshape(batch * seq, d)

    proj = _in_proj(x2, norm_w[0:1], w_in[0].astype(BF16))
    ya = _hgrn(proj, a_lower_bound, a_out_norm[0:1], batch, seq)
    qt, kn, vt = _qkv_prep(proj, jnp.tile(b_q_norm[0:1], (1, 2)),
                           jnp.tile(b_k_norm[0:1], (1, 2)), batch, seq)
    yb = _attn(qt, kn, vt, proj, b_lambda[0], b_subln[0:1], batch, seq)
    out = _out_proj(ya, yb, proj, x2, w_branch_a[0].astype(BF16),
                    w_branch_b[0].astype(BF16), w_out[0].astype(BF16))
    return out.reshape(batch, seq, d)
```

```python
import functools
import math

import jax
import jax.numpy as jnp
from jax import lax
from jax.experimental import pallas as pl
from jax.experimental.pallas import tpu as pltpu

F32 = jnp.float32
BF16 = jnp.bfloat16

EPS = 1e-6
CHUNK = 64
HEAD = 128
HEADS = 8
QK = 64
WIDTH = HEADS * HEAD
SUB = 8
LAMBDA_INIT = 0.8 - 0.6 * math.exp(-0.3 * 0)
NEG = -1e30
LOG2E = math.log2(math.e)
VT_ROWS = HEAD + 16

SEG_AQ, SEG_AF, SEG_AI, SEG_AG, SEG_BQ, SEG_BK, SEG_BV, SEG_BG = (s * HEADS for s in range(8))
SEG_GATE_A, SEG_GATE_B = 4, 5

VMEM_LIMIT = 56 * 1024 * 1024


def _dot(a, b, dims):
    return lax.dot_general(a, b, (dims, ((), ())), preferred_element_type=F32)


NN = ((1,), (0,))
NT = ((1,), (1,))
TN = ((0,), (0,))


def _in_proj_kernel(x_ref, nw_ref, w_ref, o_ref, h_ref):
    @pl.when(pl.program_id(1) == 0)
    def _():
        x = x_ref[...]
        ms = jnp.mean(x * x, axis=-1, keepdims=True)
        h_ref[...] = (x * lax.rsqrt(ms + EPS) * nw_ref[...]).astype(BF16)

    o_ref[...] = jnp.dot(h_ref[...], w_ref[...], preferred_element_type=F32)


def _in_proj(x2, norm_w, w_in_bf16, tm=1024, tn=1024):
    m, d = x2.shape
    n = w_in_bf16.shape[1]
    return pl.pallas_call(
        _in_proj_kernel,
        grid=(m // tm, n // tn),
        in_specs=[
            pl.BlockSpec((tm, d), lambda i, j: (i, 0)),
            pl.BlockSpec((1, d), lambda i, j: (0, 0)),
            pl.BlockSpec((d, tn), lambda i, j: (0, j)),
        ],
        out_specs=pl.BlockSpec((tm, tn), lambda i, j: (i, j)),
        out_shape=jax.ShapeDtypeStruct((m, n), F32),
        scratch_shapes=[pltpu.VMEM((tm, d), BF16)],
        compiler_params=pltpu.CompilerParams(
            dimension_semantics=("arbitrary", "arbitrary"), vmem_limit_bytes=VMEM_LIMIT),
        name="in_proj",
    )(x2, norm_w, w_in_bf16)


def _hgrn_kernel(alb_ref, gain_ref, aq_ref, af_ref, ai_ref, ag_ref, o_ref, st_ref, *, tt):
    @pl.when(pl.program_id(2) == 0)
    def _():
        st_ref[...] = jnp.zeros_like(st_ref)

    a = alb_ref[...]
    a0, a1 = a[0:1, :], a[1:2, :]
    amax = jnp.maximum(a0, a1)
    e0, e1 = jnp.exp(a0 - amax), jnp.exp(a1 - amax)
    lb = e0 / (e0 + e1)
    one_m_lb = 1.0 - lb
    gain = gain_ref[...]

    nch = tt // CHUNK
    row = lax.broadcasted_iota(jnp.int32, (CHUNK, CHUNK), 0)
    col = lax.broadcasted_iota(jnp.int32, (CHUNK, CHUNK), 1)
    tril = (row >= col).astype(BF16)
    same32 = (row >> 5) == (col >> 5)
    same16 = (row >> 4) == (col >> 4)
    rowt = lax.broadcasted_iota(jnp.int32, (tt, HEAD), 0)
    nblk = tt // SUB
    sub3 = lax.broadcasted_iota(jnp.int32, (nblk, SUB, HEAD), 1)
    lane3 = lax.broadcasted_iota(jnp.int32, (nblk, SUB, CHUNK), 2)
    blk3 = lax.broadcasted_iota(jnp.int32, (nblk, SUB, CHUNK), 0)

    def rows(x, c):
        return x[c * CHUNK:(c + 1) * CHUNK]

    def row_bcast(x, step, off):
        return jnp.concatenate(
            [jnp.broadcast_to(x[j * step + off:j * step + off + 1, :], (step, HEAD))
             for j in range(tt // step)], axis=0)

    z = af_ref[...]
    qa = aq_ref[...]
    v = ai_ref[...].astype(BF16)
    g = ag_ref[...]
    sig = jax.nn.sigmoid(z)
    f = lb + one_m_lb * sig
    lf = jnp.log(f)
    k = one_m_lb * (1.0 - sig)
    q = qa * jax.nn.sigmoid(qa)

    hi = lf.astype(BF16)
    r1 = lf - hi.astype(F32)
    md = r1.astype(BF16)
    lo = (r1 - md.astype(F32)).astype(BF16)
    b = jnp.concatenate(
        [_dot(tril, rows(hi, c), NN) + _dot(tril, rows(md, c), NN) + _dot(tril, rows(lo, c), NN)
         for c in range(nch)], axis=0)
    b_last = row_bcast(b, CHUNK, CHUNK - 1)
    qe = (q * jnp.exp(b)).astype(BF16)
    kd = (k * jnp.exp(b_last - b)).astype(BF16)
    dec = jnp.exp(b_last)

    def level(blk):
        half = blk // 2
        upper = (rowt & (blk - 1)) >= half
        d = b - row_bcast(b, blk, half)
        e = jnp.exp(jnp.where(upper, d, -d))
        ql = jnp.where(upper, q * e, 0.0).astype(BF16)
        kl = jnp.where(upper, 0.0, k * e).astype(BF16)
        return [_dot(rows(kl, c), rows(ql, c), NT) for c in range(nch)]

    lv64, lv32, lv16 = level(64), level(32), level(16)

    k3 = k.reshape(nblk, SUB, HEAD)
    q3 = q.reshape(nblk, SUB, HEAD)
    f3 = f.reshape(nblk, SUB, HEAD)
    gk = jnp.zeros((nblk, SUB, HEAD), F32)
    pt = jnp.zeros((nblk, SUB, CHUNK), F32)
    tcol = (blk3 & (CHUNK // SUB - 1)) * SUB
    for tl in range(SUB):
        fb = jnp.broadcast_to(f3[:, tl:tl + 1, :], (nblk, SUB, HEAD))
        qb = jnp.broadcast_to(q3[:, tl:tl + 1, :], (nblk, SUB, HEAD))
        gk = jnp.where(sub3 == tl, k3, gk * fb)
        p = jnp.sum(gk * qb, axis=2, keepdims=True)
        pt = jnp.where(lane3 == tcol + tl, p, pt)
    pt = pt.reshape(tt, CHUNK)

    st = st_ref[...]
    outs = []
    for c in range(nch):
        sc = (lv64[c] + jnp.where(same32, lv32[c], 0.0) + jnp.where(same16, lv16[c], 0.0)
              + rows(pt, c))
        intra = _dot(sc.astype(BF16), rows(v, c), TN)
        inter = _dot(rows(qe, c), st.astype(BF16), NT)
        st = st * dec[c * CHUNK:c * CHUNK + 1] + _dot(rows(v, c), rows(kd, c), TN)
        outs.append(intra + inter)
    st_ref[...] = st

    out = jnp.concatenate(outs, axis=0)
    ms = jnp.mean(out * out, axis=-1, keepdims=True)
    y = out * lax.rsqrt(ms + EPS) * gain
    o_ref[...] = (y * (g * jax.nn.sigmoid(g))).astype(o_ref.dtype)


def _hgrn(proj, a_lower_bound, a_out_norm, batch, seq, tt=512):
    m = proj.shape[0]
    nt = seq // tt

    def seg(s):
        return pl.BlockSpec((tt, HEAD), lambda b, h, t, s=s: (b * nt + t, s + h))

    return pl.pallas_call(
        functools.partial(_hgrn_kernel, tt=tt),
        grid=(batch, HEADS, nt),
        in_specs=[
            pl.BlockSpec((2, HEAD), lambda b, h, t: (0, h)),
            pl.BlockSpec((1, HEAD), lambda b, h, t: (0, 0)),
            seg(SEG_AQ), seg(SEG_AF), seg(SEG_AI), seg(SEG_AG),
        ],
        out_specs=pl.BlockSpec((tt, HEAD), lambda b, h, t: (b * nt + t, h)),
        out_shape=jax.ShapeDtypeStruct((m, WIDTH), BF16),
        scratch_shapes=[pltpu.VMEM((HEAD, HEAD), F32)],
        compiler_params=pltpu.CompilerParams(
            dimension_semantics=("arbitrary", "arbitrary", "arbitrary"),
            vmem_limit_bytes=VMEM_LIMIT),
        name="hgrn",
    )(a_lower_bound, a_out_norm, proj, proj, proj, proj)


def _qkv_prep_kernel(qg_ref, kg_ref, bq_ref, bk_ref, bv_ref, qt_ref, kn_ref, vt_ref):
    r = lax.broadcasted_iota(jnp.int32, (HEAD, HEAD), 0)
    c = lax.broadcasted_iota(jnp.int32, (HEAD, HEAD), 1)
    grp = ((r >> 6) == (c >> 6)).astype(BF16)

    def gnorm(x, gain):
        x2 = x * x
        hi = x2.astype(BF16)
        lo = (x2 - hi.astype(F32)).astype(BF16)
        ss = _dot(hi, grp, NN) + _dot(lo, grp, NN)
        return x * lax.rsqrt(ss * (1.0 / QK) + EPS) * gain

    tr = bq_ref.shape[0]
    pad_row = lax.broadcasted_iota(jnp.int32, (VT_ROWS - HEAD, tr), 0)
    ones_row = (pad_row == 0).astype(BF16)
    for h in range(HEADS):
        hs = slice(h * HEAD, (h + 1) * HEAD)
        qn = gnorm(bq_ref[:, hs], qg_ref[...]) * (QK ** -0.5 * LOG2E)
        qt_ref[0, h] = qn.T.astype(BF16)
        kn_ref[:, hs] = gnorm(bk_ref[:, hs], kg_ref[...]).astype(BF16)
        vt_ref[0, h, 0:HEAD, :] = bv_ref[:, hs].T.astype(BF16)
        vt_ref[0, h, HEAD:VT_ROWS, :] = ones_row


def _qkv_prep(proj, q_gain, k_gain, batch, seq, tr=512):
    m = proj.shape[0]
    nt = seq // tr

    def seg(s):
        return pl.BlockSpec((tr, WIDTH), lambda b, t, s=s: (b * nt + t, s // HEADS))

    def col_out(rows):
        return pl.BlockSpec((1, HEADS, rows, tr), lambda b, t: (b, 0, 0, t))

    return pl.pallas_call(
        _qkv_prep_kernel,
        grid=(batch, nt),
        in_specs=[
            pl.BlockSpec((1, HEAD), lambda b, t: (0, 0)),
            pl.BlockSpec((1, HEAD), lambda b, t: (0, 0)),
            seg(SEG_BQ), seg(SEG_BK), seg(SEG_BV),
        ],
        out_specs=[col_out(HEAD), pl.BlockSpec((tr, WIDTH), lambda b, t: (b * nt + t, 0)),
                   col_out(VT_ROWS)],
        out_shape=[jax.ShapeDtypeStruct((batch, HEADS, HEAD, seq), BF16),
                   jax.ShapeDtypeStruct((m, WIDTH), BF16),
                   jax.ShapeDtypeStruct((batch, HEADS, VT_ROWS, seq), BF16)],
        compiler_params=pltpu.CompilerParams(
            dimension_semantics=("arbitrary", "arbitrary"), vmem_limit_bytes=VMEM_LIMIT),
        name="qkv_prep",
    )(q_gain, k_gain, proj, proj, proj)


def _attn_kernel(lam_ref, sub_ref, qt_ref, k_ref, vt_ref, g_ref, o_ref,
                 sa0, sa1, pb0, pb1, acc_ref, dg_ref, *, tq, tk):
    sa = (sa0, sa1)
    pb = (pb0, pb1)
    nqb = tq // tk
    h = pl.program_id(1)
    i = pl.program_id(2)
    slope = jnp.exp2(-(jnp.zeros((1, 1), F32) + (h + 1).astype(F32))) * LOG2E
    c_hi = slope.astype(BF16).astype(F32)
    c_lo = (slope - c_hi).astype(BF16).astype(F32)
    slope = c_hi + c_lo

    @pl.when(i == 0)
    def _():
        s_i = lax.broadcasted_iota(jnp.int32, (tk, tk), 0)
        t_i = lax.broadcasted_iota(jnp.int32, (tk, tk), 1)
        corr = slope * jnp.where(s_i <= t_i, 0, 2 * (t_i - s_i)).astype(F32)
        dg_ref[...] = jnp.where((s_i >> 6) <= (t_i >> 6), corr, NEG)

    qt = qt_ref[0, 0]
    rowd = lax.broadcasted_iota(jnp.int32, (HEAD, tq), 0)
    auxq = jnp.where(rowd == 0, c_hi, jnp.where(rowd == 1, c_lo, 0.0)).astype(BF16)
    zero = jnp.zeros_like(qt)
    qx = (jnp.concatenate([jnp.where(rowd < QK, qt, zero), auxq], axis=0),
          jnp.concatenate([jnp.where(rowd >= QK, qt, zero), auxq], axis=0))
    lane_k = lax.broadcasted_iota(jnp.int32, (tk, HEAD), 1)
    row_k = lax.broadcasted_iota(jnp.int32, (tk, HEAD), 0)
    auxk = jnp.where(lane_k <= 1, row_k, 0).astype(F32).astype(BF16)

    acc_ref[...] = jnp.zeros_like(acc_ref)
    pb1[...] = jnp.zeros_like(pb1)

    units = [(mp, cb) for mp in range(2) for cb in range(nqb)]

    def step(ml, a=None, b=None, c=None):
        if a is not None:
            kb = k_ref[pl.ds(pl.multiple_of(a[0] * tk, tk), tk), :]
            kx = jnp.concatenate([kb, auxk], axis=1)
        if c is not None:
            vtb = vt_ref[0, 0, :, pl.ds(pl.multiple_of(c[0] * tk, tk), tk)]
        new_ml, alphas = list(ml), [None] * len(units)
        for u, (mp, cb) in enumerate(units):
            cs = slice(cb * tk, (cb + 1) * tk)
            if a is not None and (a[2] is None or cb >= a[2]):
                sa[a[1]][mp, :, cs] = _dot(kx, qx[mp][:, cs], NN)
            if b is not None and (b[2] is None or cb >= b[2]):
                slot, shift, d = b
                m_old = ml[u]
                st = sa[slot][mp, :, cs]
                if d is not None and cb == d:
                    st = st + dg_ref[...]
                m_new = jnp.maximum(m_old, jnp.max(st, axis=0, keepdims=True) + shift)
                alphas[u] = jnp.exp2(m_old - m_new)
                pb[slot][mp, :, cs] = jnp.exp2(st - (m_new - shift)).astype(BF16)
                new_ml[u] = m_new
            if c is not None and (c[2] is None or cb >= c[2]):
                acc_ref[mp, :, cs] = (c[3][u] * acc_ref[mp, :, cs]
                                      + _dot(vtb, pb[c[1]][mp, :, cs], NN))
        return tuple(new_ml), tuple(alphas)

    ml = tuple(jnp.full((1, tk), NEG, F32) for _ in units)
    ones = tuple(jnp.ones((1, tk), F32) for _ in units)

    step(ml, a=(0, 0, None))

    def body(p, carry):
        ml, al = carry
        for r in range(nqb):
            j = nqb * p + r
            shift = slope * (j * tk - i * tq).astype(F32)
            ml, al = step(ml, a=(j + 1, (r + 1) % 2, None), b=(r % 2, shift, None),
                          c=(jnp.maximum(j - 1, 0), (r + 1) % 2, None, al))
        return ml, al

    ml, al = lax.fori_loop(0, i, body, (ml, ones))
    jd = nqb * i
    for d in range(nqb):
        ml, al = step(ml,
                      a=(jd + d + 1, (d + 1) % 2, d + 1) if d + 1 < nqb else None,
                      b=(d % 2, slope * float(d * tk), d),
                      c=(jnp.maximum(jd + d - 1, 0), (d + 1) % 2, d - 1 if d else None, al))
    step(ml, c=(jd + nqb - 1, (nqb - 1) % 2, nqb - 1, al))

    lv = lam_ref[...]
    lam = (jnp.exp(jnp.sum(lv[0:1, :] * lv[1:2, :], keepdims=True))
           - jnp.exp(jnp.sum(lv[2:3, :] * lv[3:4, :], keepdims=True)) + LAMBDA_INIT)
    ot = (acc_ref[0, 0:HEAD, :] / acc_ref[0, HEAD:HEAD + 1, :]
          - lam * (acc_ref[1, 0:HEAD, :] / acc_ref[1, HEAD:HEAD + 1, :]))
    o = ot.T
    ms = jnp.mean(o * o, axis=-1, keepdims=True)
    y = o * lax.rsqrt(ms + EPS) * sub_ref[...] * (1.0 - LAMBDA_INIT)
    g = g_ref[...]
    o_ref[...] = (y * (g * jax.nn.sigmoid(g))).astype(o_ref.dtype)


def _attn(qt, kn, vt, proj, b_lambda, b_subln, batch, seq, tq=1024, tk=256):
    assert tq % (2 * tk) == 0 and seq % tq == 0
    m = kn.shape[0]
    nq = seq // tq
    return pl.pallas_call(
        functools.partial(_attn_kernel, tq=tq, tk=tk),
        grid=(batch, HEADS, nq),
        in_specs=[
            pl.BlockSpec((4, QK), lambda b, h, i: (0, 0)),
            pl.BlockSpec((1, HEAD), lambda b, h, i: (0, 0)),
            pl.BlockSpec((1, 1, HEAD, tq), lambda b, h, i: (b, h, 0, i)),
            pl.BlockSpec((seq, HEAD), lambda b, h, i: (b, h)),
            pl.BlockSpec((1, 1, VT_ROWS, seq), lambda b, h, i: (b, h, 0, 0)),
            pl.BlockSpec((tq, HEAD), lambda b, h, i: (b * nq + i, SEG_BG + h)),
        ],
        out_specs=pl.BlockSpec((tq, HEAD), lambda b, h, i: (b * nq + i, h)),
        out_shape=jax.ShapeDtypeStruct((m, WIDTH), BF16),
        scratch_shapes=[
            pltpu.VMEM((2, tk, tq), F32), pltpu.VMEM((2, tk, tq), F32),
            pltpu.VMEM((2, tk, tq), BF16), pltpu.VMEM((2, tk, tq), BF16),
            pltpu.VMEM((2, VT_ROWS, tq), F32), pltpu.VMEM((tk, tk), F32),
        ],
        compiler_params=pltpu.CompilerParams(
            dimension_semantics=("arbitrary", "arbitrary", "arbitrary"),
            vmem_limit_bytes=VMEM_LIMIT),
        name="attn",
    )(b_lambda, b_subln, qt, kn, vt, proj)


def _out_proj_kernel(ya_ref, yb_ref, ga_ref, gb_ref, x_ref, wa_ref, wb_ref, wo_ref, o_ref):
    ya = jnp.dot(ya_ref[...], wa_ref[...], preferred_element_type=F32)
    yb = jnp.dot(yb_ref[...], wb_ref[...], preferred_element_type=F32)
    mixed = jax.nn.sigmoid(ga_ref[...]) * ya + jax.nn.sigmoid(gb_ref[...]) * yb
    o_ref[...] = x_ref[...] + jnp.dot(mixed.astype(BF16), wo_ref[...],
                                      preferred_element_type=F32)


def _out_proj(ya, yb, proj, x2, wa, wb, wo, tm=256):
    m, d = x2.shape
    w = ya.shape[1]
    resident = functools.partial(pl.BlockSpec, pipeline_mode=pl.Buffered(1))
    return pl.pallas_call(
        _out_proj_kernel,
        grid=(m // tm,),
        in_specs=[
            pl.BlockSpec((tm, w), lambda i: (i, 0)),
            pl.BlockSpec((tm, w), lambda i: (i, 0)),
            pl.BlockSpec((tm, d), lambda i: (i, SEG_GATE_A)),
            pl.BlockSpec((tm, d), lambda i: (i, SEG_GATE_B)),
            pl.BlockSpec((tm, d), lambda i: (i, 0)),
            resident((w, d), lambda i: (0, 0)),
            resident((w, d), lambda i: (0, 0)),
            resident((d, d), lambda i: (0, 0)),
        ],
        out_specs=pl.BlockSpec((tm, d), lambda i: (i, 0)),
        out_shape=jax.ShapeDtypeStruct((m, d), F32),
        compiler_params=pltpu.CompilerParams(
            dimension_semantics=("arbitrary",), vmem_limit_bytes=VMEM_LIMIT),
        name="out_proj",
    )(ya, yb, proj, proj, x2, wa, wb, wo)


def kernel(x, norm_w, w_in, a_lower_bound, a_out_norm, b_q_norm, b_k_norm, b_lambda,
           b_subln, w_branch_a, w_branch_b, w_out):
    batch, seq, d = x.shape
    assert norm_w.shape[0] == 1 and a_lower_bound.shape == (2, WIDTH)
    assert d == 2 * WIDTH and seq % 512 == 0
    x2 = x.reshape(batch * seq, d)

    proj = _in_proj(x2, norm_w[0:1], w_in[0].astype(BF16))
    ya = _hgrn(proj, a_lower_bound, a_out_norm[0:1], batch, seq)
    qt, kn, vt = _qkv_prep(proj, jnp.tile(b_q_norm[0:1], (1, 2)),
                           jnp.tile(b_k_norm[0:1], (1, 2)), batch, seq)
    yb = _attn(qt, kn, vt, proj, b_lambda[0], b_subln[0:1], batch, seq)
    out = _out_proj(ya, yb, proj, x2, w_branch_a[0].astype(BF16),
                    w_branch_b[0].astype(BF16), w_out[0].astype(BF16))
    return out.reshape(batch, seq, d)
```

```python
import functools
import math

import jax
import jax.numpy as jnp
from jax import lax
from jax.experimental import pallas as pl
from jax.experimental.pallas import tpu as pltpu

F32 = jnp.float32
BF16 = jnp.bfloat16

EPS = 1e-6
CHUNK = 64
HEAD = 128
HEADS = 8
QK = 64
WIDTH = HEADS * HEAD
SUB = 8
LAMBDA_INIT = 0.8 - 0.6 * math.exp(-0.3 * 0)
NEG = -1e30
LOG2E = math.log2(math.e)
VT_ROWS = HEAD + 16

SEG_AQ, SEG_AF, SEG_AI, SEG_AG, SEG_BQ, SEG_BK, SEG_BV, SEG_BG = (s * HEADS for s in range(8))
SEG_GATE_A, SEG_GATE_B = 4, 5

VMEM_LIMIT = 56 * 1024 * 1024


def _dot(a, b, dims):
    return lax.dot_general(a, b, (dims, ((), ())), preferred_element_type=F32)


NN = ((1,), (0,))
NT = ((1,), (1,))
TN = ((0,), (0,))


def _in_proj_kernel(x_ref, nw_ref, w_ref, o_ref, h_ref):
    @pl.when(pl.program_id(1) == 0)
    def _():
        x = x_ref[...]
        ms = jnp.mean(x * x, axis=-1, keepdims=True)
        h_ref[...] = (x * lax.rsqrt(ms + EPS) * nw_ref[...]).astype(BF16)

    o_ref[...] = jnp.dot(h_ref[...], w_ref[...], preferred_element_type=F32)


def _in_proj(x2, norm_w, w_in_bf16, tm=1024, tn=1024):
    m, d = x2.shape
    n = w_in_bf16.shape[1]
    return pl.pallas_call(
        _in_proj_kernel,
        grid=(m // tm, n // tn),
        in_specs=[
            pl.BlockSpec((tm, d), lambda i, j: (i, 0)),
            pl.BlockSpec((1, d), lambda i, j: (0, 0)),
            pl.BlockSpec((d, tn), lambda i, j: (0, j)),
        ],
        out_specs=pl.BlockSpec((tm, tn), lambda i, j: (i, j)),
        out_shape=jax.ShapeDtypeStruct((m, n), F32),
        scratch_shapes=[pltpu.VMEM((tm, d), BF16)],
        compiler_params=pltpu.CompilerParams(
            dimension_semantics=("arbitrary", "arbitrary"), vmem_limit_bytes=VMEM_LIMIT),
        name="in_proj",
    )(x2, norm_w, w_in_bf16)


def _hgrn_kernel(alb_ref, gain_ref, aq_ref, af_ref, ai_ref, ag_ref, o_ref, st_ref, *, tt):
    @pl.when(pl.program_id(2) == 0)
    def _():
        st_ref[...] = jnp.zeros_like(st_ref)

    a = alb_ref[...]
    a0, a1 = a[0:1, :], a[1:2, :]
    amax = jnp.maximum(a0, a1)
    e0, e1 = jnp.exp(a0 - amax), jnp.exp(a1 - amax)
    lb = e0 / (e0 + e1)
    one_m_lb = 1.0 - lb
    gain = gain_ref[...]

    nch = tt // CHUNK
    row = lax.broadcasted_iota(jnp.int32, (CHUNK, CHUNK), 0)
    col = lax.broadcasted_iota(jnp.int32, (CHUNK, CHUNK), 1)
    tril = (row >= col).astype(BF16)
    same32 = (row >> 5) == (col >> 5)
    same16 = (row >> 4) == (col >> 4)
    rowt = lax.broadcasted_iota(jnp.int32, (tt, HEAD), 0)
    nblk = tt // SUB
    sub3 = lax.broadcasted_iota(jnp.int32, (nblk, SUB, HEAD), 1)
    lane3 = lax.broadcasted_iota(jnp.int32, (nblk, SUB, CHUNK), 2)
    blk3 = lax.broadcasted_iota(jnp.int32, (nblk, SUB, CHUNK), 0)

    def rows(x, c):
        return x[c * CHUNK:(c + 1) * CHUNK]

    def row_bcast(x, step, off):
        return jnp.concatenate(
            [jnp.broadcast_to(x[j * step + off:j * step + off + 1, :], (step, HEAD))
             for j in range(tt // step)], axis=0)

    z = af_ref[...]
    qa = aq_ref[...]
    v = ai_ref[...].astype(BF16)
    g = ag_ref[...]
    sig = jax.nn.sigmoid(z)
    f = lb + one_m_lb * sig
    lf = jnp.log(f)
    k = one_m_lb * (1.0 - sig)
    q = qa * jax.nn.sigmoid(qa)

    hi = lf.astype(BF16)
    r1 = lf - hi.astype(F32)
    md = r1.astype(BF16)
    lo = (r1 - md.astype(F32)).astype(BF16)
    b = jnp.concatenate(
        [_dot(tril, rows(hi, c), NN) + _dot(tril, rows(md, c), NN) + _dot(tril, rows(lo, c), NN)
         for c in range(nch)], axis=0)
    b_last = row_bcast(b, CHUNK, CHUNK - 1)
    qe = (q * jnp.exp(b)).astype(BF16)
    kd = (k * jnp.exp(b_last - b)).astype(BF16)
    dec = jnp.exp(b_last)

    def level(blk):
        half = blk // 2
        upper = (rowt & (blk - 1)) >= half
        d = b - row_bcast(b, blk, half)
        e = jnp.exp(jnp.where(upper, d, -d))
        ql = jnp.where(upper, q * e, 0.0).astype(BF16)
        kl = jnp.where(upper, 0.0, k * e).astype(BF16)
        return [_dot(rows(kl, c), rows(ql, c), NT) for c in range(nch)]

    lv64, lv32, lv16 = level(64), level(32), level(16)

    k3 = k.reshape(nblk, SUB, HEAD)
    q3 = q.reshape(nblk, SUB, HEAD)
    f3 = f.reshape(nblk, SUB, HEAD)
    gk = jnp.zeros((nblk, SUB, HEAD), F32)
    pt = jnp.zeros((nblk, SUB, CHUNK), F32)
    tcol = (blk3 & (CHUNK // SUB - 1)) * SUB
    for tl in range(SUB):
        fb = jnp.broadcast_to(f3[:, tl:tl + 1, :], (nblk, SUB, HEAD))
        qb = jnp.broadcast_to(q3[:, tl:tl + 1, :], (nblk, SUB, HEAD))
        gk = jnp.where(sub3 == tl, k3, gk * fb)
        p = jnp.sum(gk * qb, axis=2, keepdims=True)
        pt = jnp.where(lane3 == tcol + tl, p, pt)
    pt = pt.reshape(tt, CHUNK)

    st = st_ref[...]
    outs = []
    for c in range(nch):
        sc = (lv64[c] + jnp.where(same32, lv32[c], 0.0) + jnp.where(same16, lv16[c], 0.0)
              + rows(pt, c))
        intra = _dot(sc.astype(BF16), rows(v, c), TN)
        inter = _dot(rows(qe, c), st.astype(BF16), NT)
        st = st * dec[c * CHUNK:c * CHUNK + 1] + _dot(rows(v, c), rows(kd, c), TN)
        outs.append(intra + inter)
    st_ref[...] = st

    out = jnp.concatenate(outs, axis=0)
    ms = jnp.mean(out * out, axis=-1, keepdims=True)
    y = out * lax.rsqrt(ms + EPS) * gain
    o_ref[...] = (y * (g * jax.nn.sigmoid(g))).astype(o_ref.dtype)


def _hgrn(proj, a_lower_bound, a_out_norm, batch, seq, tt=512):
    m = proj.shape[0]
    nt = seq // tt

    def seg(s):
        return pl.BlockSpec((tt, HEAD), lambda b, h, t, s=s: (b * nt + t, s + h))

    return pl.pallas_call(
        functools.partial(_hgrn_kernel, tt=tt),
        grid=(batch, HEADS, nt),
        in_specs=[
            pl.BlockSpec((2, HEAD), lambda b, h, t: (0, h)),
            pl.BlockSpec((1, HEAD), lambda b, h, t: (0, 0)),
            seg(SEG_AQ), seg(SEG_AF), seg(SEG_AI), seg(SEG_AG),
        ],
        out_specs=pl.BlockSpec((tt, HEAD), lambda b, h, t: (b * nt + t, h)),
        out_shape=jax.ShapeDtypeStruct((m, WIDTH), BF16),
        scratch_shapes=[pltpu.VMEM((HEAD, HEAD), F32)],
        compiler_params=pltpu.CompilerParams(
            dimension_semantics=("arbitrary", "arbitrary", "arbitrary"),
            vmem_limit_bytes=VMEM_LIMIT),
        name="hgrn",
    )(a_lower_bound, a_out_norm, proj, proj, proj, proj)


def _qkv_prep_kernel(qg_ref, kg_ref, bq_ref, bk_ref, bv_ref, qt_ref, kn_ref, vt_ref):
    r = lax.broadcasted_iota(jnp.int32, (HEAD, HEAD), 0)
    c = lax.broadcasted_iota(jnp.int32, (HEAD, HEAD), 1)
    grp = ((r >> 6) == (c >> 6)).astype(BF16)

    def gnorm(x, gain):
        x2 = x * x
        hi = x2.astype(BF16)
        lo = (x2 - hi.astype(F32)).astype(BF16)
        ss = _dot(hi, grp, NN) + _dot(lo, grp, NN)
        return x * lax.rsqrt(ss * (1.0 / QK) + EPS) * gain

    tr = bq_ref.shape[0]
    pad_row = lax.broadcasted_iota(jnp.int32, (VT_ROWS - HEAD, tr), 0)
    ones_row = (pad_row == 0).astype(BF16)
    for h in range(HEADS):
        hs = slice(h * HEAD, (h + 1) * HEAD)
        qn = gnorm(bq_ref[:, hs], qg_ref[...]) * (QK ** -0.5 * LOG2E)
        qt_ref[0, h] = qn.T.astype(BF16)
        kn_ref[:, hs] = gnorm(bk_ref[:, hs], kg_ref[...]).astype(BF16)
        vt_ref[0, h, 0:HEAD, :] = bv_ref[:, hs].T.astype(BF16)
        vt_ref[0, h, HEAD:VT_ROWS, :] = ones_row


def _qkv_prep(proj, q_gain, k_gain, batch, seq, tr=512):
    m = proj.shape[0]
    nt = seq // tr

    def seg(s):
        return pl.BlockSpec((tr, WIDTH), lambda b, t, s=s: (b * nt + t, s // HEADS))

    def col_out(rows):
        return pl.BlockSpec((1, HEADS, rows, tr), lambda b, t: (b, 0, 0, t))

    return pl.pallas_call(
        _qkv_prep_kernel,
        grid=(batch, nt),
        in_specs=[
            pl.BlockSpec((1, HEAD), lambda b, t: (0, 0)),
            pl.BlockSpec((1, HEAD), lambda b, t: (0, 0)),
            seg(SEG_BQ), seg(SEG_BK), seg(SEG_BV),
        ],
        out_specs=[col_out(HEAD), pl.BlockSpec((tr, WIDTH), lambda b, t: (b * nt + t, 0)),
                   col_out(VT_ROWS)],
        out_shape=[jax.ShapeDtypeStruct((batch, HEADS, HEAD, seq), BF16),
                   jax.ShapeDtypeStruct((m, WIDTH), BF16),
                   jax.ShapeDtypeStruct((batch, HEADS, VT_ROWS, seq), BF16)],
        compiler_params=pltpu.CompilerParams(
            dimension_semantics=("arbitrary", "arbitrary"), vmem_limit_bytes=VMEM_LIMIT),
        name="qkv_prep",
    )(q_gain, k_gain, proj, proj, proj)


def _attn_kernel(lam_ref, sub_ref, qt_ref, k_ref, vt_ref, g_ref, o_ref,
                 sa0, sa1, pb0, pb1, acc_ref, dg_ref, *, tq, tk):
    sa = (sa0, sa1)
    pb = (pb0, pb1)
    nqb = tq // tk
    h = pl.program_id(1)
    i = pl.program_id(2)
    slope = jnp.exp2(-(jnp.zeros((1, 1), F32) + (h + 1).astype(F32))) * LOG2E
    c_hi = slope.astype(BF16).astype(F32)
    c_lo = (slope - c_hi).astype(BF16).astype(F32)
    slope = c_hi + c_lo

    @pl.when(i == 0)
    def _():
        s_i = lax.broadcasted_iota(jnp.int32, (tk, tk), 0)
        t_i = lax.broadcasted_iota(jnp.int32, (tk, tk), 1)
        corr = slope * jnp.where(s_i <= t_i, 0, 2 * (t_i - s_i)).astype(F32)
        dg_ref[...] = jnp.where((s_i >> 6) <= (t_i >> 6), corr, NEG)

    qt = qt_ref[0, 0]
    rowd = lax.broadcasted_iota(jnp.int32, (HEAD, tq), 0)
    auxq = jnp.where(rowd == 0, c_hi, jnp.where(rowd == 1, c_lo, 0.0)).astype(BF16)
    zero = jnp.zeros_like(qt)
    qx = (jnp.concatenate([jnp.where(rowd < QK, qt, zero), auxq], axis=0),
          jnp.concatenate([jnp.where(rowd >= QK, qt, zero), auxq], axis=0))
    lane_k = lax.broadcasted_iota(jnp.int32, (tk, HEAD), 1)
    row_k = lax.broadcasted_iota(jnp.int32, (tk, HEAD), 0)
    auxk = jnp.where(lane_k <= 1, row_k, 0).astype(F32).astype(BF16)

    acc_ref[...] = jnp.zeros_like(acc_ref)
    pb1[...] = jnp.zeros_like(pb1)

    units = [(mp, cb) for mp in range(2) for cb in range(nqb)]

    def step(ml, a=None, b=None, c=None):
        if a is not None:
            kb = k_ref[pl.ds(pl.multiple_of(a[0] * tk, tk), tk), :]
            kx = jnp.concatenate([kb, auxk], axis=1)
        if c is not None:
            vtb = vt_ref[0, 0, :, pl.ds(pl.multiple_of(c[0] * tk, tk), tk)]
        new_ml, alphas = list(ml), [None] * len(units)
        for u, (mp, cb) in enumerate(units):
            cs = slice(cb * tk, (cb + 1) * tk)
            if a is not None and (a[2] is None or cb >= a[2]):
                sa[a[1]][mp, :, cs] = _dot(kx, qx[mp][:, cs], NN)
            if b is not None and (b[2] is None or cb >= b[2]):
                slot, shift, d = b
                m_old = ml[u]
                st = sa[slot][mp, :, cs]
                if d is not None and cb == d:
                    st = st + dg_ref[...]
                m_new = jnp.maximum(m_old, jnp.max(st, axis=0, keepdims=True) + shift)
                alphas[u] = jnp.exp2(m_old - m_new)
                pb[slot][mp, :, cs] = jnp.exp2(st - (m_new - shift)).astype(BF16)
                new_ml[u] = m_new
            if c is not None and (c[2] is None or cb >= c[2]):
                acc_ref[mp, :, cs] = (c[3][u] * acc_ref[mp, :, cs]
                                      + _dot(vtb, pb[c[1]][mp, :, cs], NN))
        return tuple(new_ml), tuple(alphas)

    ml = tuple(jnp.full((1, tk), NEG, F32) for _ in units)
    ones = tuple(jnp.ones((1, tk), F32) for _ in units)

    step(ml, a=(0, 0, None))

    def body(p, carry):
        ml, al = carry
        for r in range(nqb):
            j = nqb * p + r
            shift = slope * (j * tk - i * tq).astype(F32)
            ml, al = step(ml, a=(j + 1, (r + 1) % 2, None), b=(r % 2, shift, None),
                          c=(jnp.maximum(j - 1, 0), (r + 1) % 2, None, al))
        return ml, al

    ml, al = lax.fori_loop(0, i, body, (ml, ones))
    jd = nqb * i
    for d in range(nqb):
        ml, al = step(ml,
                      a=(jd + d + 1, (d + 1) % 2, d + 1) if d + 1 < nqb else None,
                      b=(d % 2, slope * float(d * tk), d),
                      c=(jnp.maximum(jd + d - 1, 0), (d + 1) % 2, d - 1 if d else None, al))
    step(ml, c=(jd + nqb - 1, (nqb - 1) % 2, nqb - 1, al))

    lv = lam_ref[...]
    lam = (jnp.exp(jnp.sum(lv[0:1, :] * lv[1:2, :], keepdims=True))
           - jnp.exp(jnp.sum(lv[2:3, :] * lv[3:4, :], keepdims=True)) + LAMBDA_INIT)
    ot = (acc_ref[0, 0:HEAD, :] / acc_ref[0, HEAD:HEAD + 1, :]
          - lam * (acc_ref[1, 0:HEAD, :] / acc_ref[1, HEAD:HEAD + 1, :]))
    o = ot.T
    ms = jnp.mean(o * o, axis=-1, keepdims=True)
    y = o * lax.rsqrt(ms + EPS) * sub_ref[...] * (1.0 - LAMBDA_INIT)
    g = g_ref[...]
    o_ref[...] = (y * (g * jax.nn.sigmoid(g))).astype(o_ref.dtype)


def _attn(qt, kn, vt, proj, b_lambda, b_subln, batch, seq, tq=2048, tk=256):
    assert tq % (2 * tk) == 0 and seq % tq == 0
    m = kn.shape[0]
    nq = seq // tq
    return pl.pallas_call(
        functools.partial(_attn_kernel, tq=tq, tk=tk),
        grid=(batch, HEADS, nq),
        in_specs=[
            pl.BlockSpec((4, QK), lambda b, h, i: (0, 0)),
            pl.BlockSpec((1, HEAD), lambda b, h, i: (0, 0)),
            pl.BlockSpec((1, 1, HEAD, tq), lambda b, h, i: (b, h, 0, i)),
            pl.BlockSpec((seq, HEAD), lambda b, h, i: (b, h)),
            pl.BlockSpec((1, 1, VT_ROWS, seq), lambda b, h, i: (b, h, 0, 0)),
            pl.BlockSpec((tq, HEAD), lambda b, h, i: (b * nq + i, SEG_BG + h)),
        ],
        out_specs=pl.BlockSpec((tq, HEAD), lambda b, h, i: (b * nq + i, h)),
        out_shape=jax.ShapeDtypeStruct((m, WIDTH), BF16),
        scratch_shapes=[
            pltpu.VMEM((2, tk, tq), F32), pltpu.VMEM((2, tk, tq), F32),
            pltpu.VMEM((2, tk, tq), BF16), pltpu.VMEM((2, tk, tq), BF16),
            pltpu.VMEM((2, VT_ROWS, tq), F32), pltpu.VMEM((tk, tk), F32),
        ],
        compiler_params=pltpu.CompilerParams(
            dimension_semantics=("arbitrary", "arbitrary", "arbitrary"),
            vmem_limit_bytes=VMEM_LIMIT),
        name="attn",
    )(b_lambda, b_subln, qt, kn, vt, proj)


def _out_proj_kernel(ya_ref, yb_ref, ga_ref, gb_ref, x_ref, wa_ref, wb_ref, wo_ref, o_ref):
    ya = jnp.dot(ya_ref[...], wa_ref[...], preferred_element_type=F32)
    yb = jnp.dot(yb_ref[...], wb_ref[...], preferred_element_type=F32)
    mixed = jax.nn.sigmoid(ga_ref[...]) * ya + jax.nn.sigmoid(gb_ref[...]) * yb
    o_ref[...] = x_ref[...] + jnp.dot(mixed.astype(BF16), wo_ref[...],
                                      preferred_element_type=F32)


def _out_proj(ya, yb, proj, x2, wa, wb, wo, tm=256):
    m, d = x2.shape
    w = ya.shape[1]
    resident = functools.partial(pl.BlockSpec, pipeline_mode=pl.Buffered(1))
    return pl.pallas_call(
        _out_proj_kernel,
        grid=(m // tm,),
        in_specs=[
            pl.BlockSpec((tm, w), lambda i: (i, 0)),
            pl.BlockSpec((tm, w), lambda i: (i, 0)),
            pl.BlockSpec((tm, d), lambda i: (i, SEG_GATE_A)),
            pl.BlockSpec((tm, d), lambda i: (i, SEG_GATE_B)),
            pl.BlockSpec((tm, d), lambda i: (i, 0)),
            resident((w, d), lambda i: (0, 0)),
            resident((w, d), lambda i: (0, 0)),
            resident((d, d), lambda i: (0, 0)),
        ],
        out_specs=pl.BlockSpec((tm, d), lambda i: (i, 0)),
        out_shape=jax.ShapeDtypeStruct((m, d), F32),
        compiler_params=pltpu.CompilerParams(
            dimension_semantics=("arbitrary",), vmem_limit_bytes=VMEM_LIMIT),
        name="out_proj",
    )(ya, yb, proj, proj, x2, wa, wb, wo)


def kernel(x, norm_w, w_in, a_lower_bound, a_out_norm, b_q_norm, b_k_norm, b_lambda,
           b_subln, w_branch_a, w_branch_b, w_out):
    batch, seq, d = x.shape
    assert norm_w.shape[0] == 1 and a_lower_bound.shape == (2, WIDTH)
    assert d == 2 * WIDTH and seq % 512 == 0
    x2 = x.reshape(batch * seq, d)

    proj = _in_proj(x2, norm_w[0:1], w_in[0].astype(BF16))
    ya = _hgrn(proj, a_lower_bound, a_out_norm[0:1], batch, seq)
    qt, kn, vt = _qkv_prep(proj, jnp.tile(b_q_norm[0:1], (1, 2)),
                           jnp.tile(b_k_norm[0:1], (1, 2)), batch, seq)
    yb = _attn(qt, kn, vt, proj, b_lambda[0], b_subln[0:1], batch, seq)
    out = _out_proj(ya, yb, proj, x2, w_branch_a[0].astype(BF16),
                    w_branch_b[0].astype(BF16), w_out[0].astype(BF16))
    return out.reshape(batch, seq, d)
```

```python
import functools
import math

import jax
import jax.numpy as jnp
from jax import lax
from jax.experimental import pallas as pl
from jax.experimental.pallas import tpu as pltpu

F32 = jnp.float32
BF16 = jnp.bfloat16

EPS = 1e-6
CHUNK = 64
HEAD = 128
HEADS = 8
QK = 64
WIDTH = HEADS * HEAD
SUB = 8
LAMBDA_INIT = 0.8 - 0.6 * math.exp(-0.3 * 0)
NEG = -1e30
LOG2E = math.log2(math.e)
VT_ROWS = HEAD + 16

SEG_AQ, SEG_AF, SEG_AI, SEG_AG, SEG_BQ, SEG_BK, SEG_BV, SEG_BG = (s * HEADS for s in range(8))
SEG_GATE_A, SEG_GATE_B = 4, 5

VMEM_LIMIT = 56 * 1024 * 1024


def _dot(a, b, dims):
    return lax.dot_general(a, b, (dims, ((), ())), preferred_element_type=F32)


NN = ((1,), (0,))
NT = ((1,), (1,))
TN = ((0,), (0,))


def _in_proj_kernel(x_ref, nw_ref, w_ref, o_ref, h_ref):
    @pl.when(pl.program_id(1) == 0)
    def _():
        x = x_ref[...]
        ms = jnp.mean(x * x, axis=-1, keepdims=True)
        h_ref[...] = (x * lax.rsqrt(ms + EPS) * nw_ref[...]).astype(BF16)

    o_ref[...] = jnp.dot(h_ref[...], w_ref[...], preferred_element_type=F32)


def _in_proj(x2, norm_w, w_in_bf16, tm=1024, tn=1024):
    m, d = x2.shape
    n = w_in_bf16.shape[1]
    return pl.pallas_call(
        _in_proj_kernel,
        grid=(m // tm, n // tn),
        in_specs=[
            pl.BlockSpec((tm, d), lambda i, j: (i, 0)),
            pl.BlockSpec((1, d), lambda i, j: (0, 0)),
            pl.BlockSpec((d, tn), lambda i, j: (0, j)),
        ],
        out_specs=pl.BlockSpec((tm, tn), lambda i, j: (i, j)),
        out_shape=jax.ShapeDtypeStruct((m, n), F32),
        scratch_shapes=[pltpu.VMEM((tm, d), BF16)],
        compiler_params=pltpu.CompilerParams(
            dimension_semantics=("arbitrary", "arbitrary"), vmem_limit_bytes=VMEM_LIMIT),
        name="in_proj",
    )(x2, norm_w, w_in_bf16)


def _hgrn_kernel(alb_ref, gain_ref, aq_ref, af_ref, ai_ref, ag_ref, o_ref, st_ref, *, tt):
    @pl.when(pl.program_id(2) == 0)
    def _():
        st_ref[...] = jnp.zeros_like(st_ref)

    a = alb_ref[...]
    a0, a1 = a[0:1, :], a[1:2, :]
    amax = jnp.maximum(a0, a1)
    e0, e1 = jnp.exp(a0 - amax), jnp.exp(a1 - amax)
    lb = e0 / (e0 + e1)
    one_m_lb = 1.0 - lb
    gain = gain_ref[...]

    nch = tt // CHUNK
    row = lax.broadcasted_iota(jnp.int32, (CHUNK, CHUNK), 0)
    col = lax.broadcasted_iota(jnp.int32, (CHUNK, CHUNK), 1)
    tril = (row >= col).astype(BF16)
    same32 = (row >> 5) == (col >> 5)
    same16 = (row >> 4) == (col >> 4)
    rowt = lax.broadcasted_iota(jnp.int32, (tt, HEAD), 0)
    nb = CHUNK // SUB
    sub4 = lax.broadcasted_iota(jnp.int32, (1, 1, SUB, HEAD), 2)
    rel4 = (lax.broadcasted_iota(jnp.int32, (1, nb, SUB, CHUNK), 3)
            - SUB * lax.broadcasted_iota(jnp.int32, (1, nb, SUB, CHUNK), 1))

    def rows(x, c):
        return x[c * CHUNK:(c + 1) * CHUNK]

    def row_bcast(x, step, off):
        return jnp.concatenate(
            [jnp.broadcast_to(x[j * step + off:j * step + off + 1, :], (step, HEAD))
             for j in range(tt // step)], axis=0)

    z = af_ref[...]
    qa = aq_ref[...]
    v = ai_ref[...].astype(BF16)
    g = ag_ref[...]
    sig = jax.nn.sigmoid(z)
    f = lb + one_m_lb * sig
    lf = jnp.log(f)
    k = one_m_lb * (1.0 - sig)
    q = qa * jax.nn.sigmoid(qa)

    hi = lf.astype(BF16)
    r1 = lf - hi.astype(F32)
    md = r1.astype(BF16)
    lo = (r1 - md.astype(F32)).astype(BF16)
    b = jnp.concatenate(
        [_dot(tril, rows(hi, c), NN) + _dot(tril, rows(md, c), NN) + _dot(tril, rows(lo, c), NN)
         for c in range(nch)], axis=0)
    b_last = row_bcast(b, CHUNK, CHUNK - 1)
    qe = (q * jnp.exp(b)).astype(BF16)
    kd = (k * jnp.exp(b_last - b)).astype(BF16)
    dec = jnp.exp(b_last)

    def level(blk):
        half = blk // 2
        upper = (rowt & (blk - 1)) >= half
        d = b - row_bcast(b, blk, half)
        e = jnp.exp(jnp.where(upper, d, -d))
        ql = jnp.where(upper, q * e, 0.0).astype(BF16)
        kl = jnp.where(upper, 0.0, k * e).astype(BF16)
        return [_dot(rows(kl, c), rows(ql, c), NT) for c in range(nch)]

    lv64, lv32, lv16 = level(64), level(32), level(16)

    shape4 = (nch, nb, SUB, HEAD)
    k4, q4, f4 = k.reshape(shape4), q.reshape(shape4), f.reshape(shape4)
    gk = jnp.zeros(shape4, F32)
    pt = jnp.zeros((nch, nb, SUB, CHUNK), F32)
    for tl in range(SUB):
        fb = jnp.broadcast_to(f4[:, :, tl:tl + 1, :], shape4)
        qb = jnp.broadcast_to(q4[:, :, tl:tl + 1, :], shape4)
        gk = jnp.where(sub4 == tl, k4, gk * fb)
        p = jnp.sum(gk * qb, axis=3, keepdims=True)
        pt = jnp.where(rel4 == tl, p, pt)
    pt = pt.reshape(tt, CHUNK)

    intras = []
    for c in range(nch):
        sc = (lv64[c] + jnp.where(same32, lv32[c], 0.0) + jnp.where(same16, lv16[c], 0.0)
              + rows(pt, c))
        intras.append(_dot(sc.astype(BF16), rows(v, c), TN))
    updates = [_dot(rows(v, c), rows(kd, c), TN) for c in range(nch)]
    st = st_ref[...]
    outs = []
    for c in range(nch):
        outs.append(intras[c] + _dot(rows(qe, c), st.astype(BF16), NT))
        st = st * dec[c * CHUNK:c * CHUNK + 1] + updates[c]
    st_ref[...] = st

    out = jnp.concatenate(outs, axis=0)
    ms = jnp.mean(out * out, axis=-1, keepdims=True)
    y = out * lax.rsqrt(ms + EPS) * gain
    o_ref[...] = (y * (g * jax.nn.sigmoid(g))).astype(o_ref.dtype)


def _hgrn(proj, a_lower_bound, a_out_norm, batch, seq, tt=512):
    m = proj.shape[0]
    nt = seq // tt

    def seg(s):
        return pl.BlockSpec((tt, HEAD), lambda b, h, t, s=s: (b * nt + t, s + h))

    return pl.pallas_call(
        functools.partial(_hgrn_kernel, tt=tt),
        grid=(batch, HEADS, nt),
        in_specs=[
            pl.BlockSpec((2, HEAD), lambda b, h, t: (0, h)),
            pl.BlockSpec((1, HEAD), lambda b, h, t: (0, 0)),
            seg(SEG_AQ), seg(SEG_AF), seg(SEG_AI), seg(SEG_AG),
        ],
        out_specs=pl.BlockSpec((tt, HEAD), lambda b, h, t: (b * nt + t, h)),
        out_shape=jax.ShapeDtypeStruct((m, WIDTH), BF16),
        scratch_shapes=[pltpu.VMEM((HEAD, HEAD), F32)],
        compiler_params=pltpu.CompilerParams(
            dimension_semantics=("arbitrary", "arbitrary", "arbitrary"),
            vmem_limit_bytes=VMEM_LIMIT),
        name="hgrn",
    )(a_lower_bound, a_out_norm, proj, proj, proj, proj)


def _qkv_prep_kernel(qg_ref, kg_ref, bq_ref, bk_ref, bv_ref, qt_ref, kn_ref, vt_ref):
    r = lax.broadcasted_iota(jnp.int32, (HEAD, HEAD), 0)
    c = lax.broadcasted_iota(jnp.int32, (HEAD, HEAD), 1)
    grp = ((r >> 6) == (c >> 6)).astype(BF16)

    def gnorm(x, gain):
        x2 = x * x
        hi = x2.astype(BF16)
        lo = (x2 - hi.astype(F32)).astype(BF16)
        ss = _dot(hi, grp, NN) + _dot(lo, grp, NN)
        return x * lax.rsqrt(ss * (1.0 / QK) + EPS) * gain

    tr = bq_ref.shape[0]
    pad_row = lax.broadcasted_iota(jnp.int32, (VT_ROWS - HEAD, tr), 0)
    ones_row = (pad_row == 0).astype(BF16)
    for h in range(HEADS):
        hs = slice(h * HEAD, (h + 1) * HEAD)
        qn = gnorm(bq_ref[:, hs], qg_ref[...]) * (QK ** -0.5 * LOG2E)
        qt_ref[0, h] = qn.T.astype(BF16)
        kn_ref[:, hs] = gnorm(bk_ref[:, hs], kg_ref[...]).astype(BF16)
        vt_ref[0, h, 0:HEAD, :] = bv_ref[:, hs].T.astype(BF16)
        vt_ref[0, h, HEAD:VT_ROWS, :] = ones_row


def _qkv_prep(proj, q_gain, k_gain, batch, seq, tr=512):
    m = proj.shape[0]
    nt = seq // tr

    def seg(s):
        return pl.BlockSpec((tr, WIDTH), lambda b, t, s=s: (b * nt + t, s // HEADS))

    def col_out(rows):
        return pl.BlockSpec((1, HEADS, rows, tr), lambda b, t: (b, 0, 0, t))

    return pl.pallas_call(
        _qkv_prep_kernel,
        grid=(batch, nt),
        in_specs=[
            pl.BlockSpec((1, HEAD), lambda b, t: (0, 0)),
            pl.BlockSpec((1, HEAD), lambda b, t: (0, 0)),
            seg(SEG_BQ), seg(SEG_BK), seg(SEG_BV),
        ],
        out_specs=[col_out(HEAD), pl.BlockSpec((tr, WIDTH), lambda b, t: (b * nt + t, 0)),
                   col_out(VT_ROWS)],
        out_shape=[jax.ShapeDtypeStruct((batch, HEADS, HEAD, seq), BF16),
                   jax.ShapeDtypeStruct((m, WIDTH), BF16),
                   jax.ShapeDtypeStruct((batch, HEADS, VT_ROWS, seq), BF16)],
        compiler_params=pltpu.CompilerParams(
            dimension_semantics=("arbitrary", "arbitrary"), vmem_limit_bytes=VMEM_LIMIT),
        name="qkv_prep",
    )(q_gain, k_gain, proj, proj, proj)


def _attn_kernel(lam_ref, sub_ref, qt_ref, k_ref, vt_ref, g_ref, o_ref,
                 sa0, sa1, pb0, pb1, acc_ref, qx_ref, dg_ref, *, seq, tk):
    sa = (sa0, sa1)
    pb = (pb0, pb1)
    ncb = seq // tk
    nslot = ncb // 2
    h = pl.program_id(1)
    slope = jnp.exp2(-(jnp.zeros((1, 1), F32) + (h + 1).astype(F32))) * LOG2E
    c_hi = slope.astype(BF16).astype(F32)
    c_lo = (slope - c_hi).astype(BF16).astype(F32)
    slope = c_hi + c_lo

    s_i = lax.broadcasted_iota(jnp.int32, (tk, tk), 0)
    t_i = lax.broadcasted_iota(jnp.int32, (tk, tk), 1)
    corr = slope * jnp.where(s_i <= t_i, 0, 2 * (t_i - s_i)).astype(F32)
    dg_ref[...] = jnp.where((s_i >> 6) <= (t_i >> 6), corr, NEG)

    rowd = lax.broadcasted_iota(jnp.int32, (HEAD, tk), 0)
    auxq = jnp.where(rowd == 0, c_hi, jnp.where(rowd == 1, c_lo, 0.0)).astype(BF16)
    for cb in range(ncb):
        cs = slice(cb * tk, (cb + 1) * tk)
        qs = qt_ref[0, 0, :, cs]
        zero = jnp.zeros_like(qs)
        qx_ref[0, 0:HEAD, cs] = jnp.where(rowd < QK, qs, zero)
        qx_ref[1, 0:HEAD, cs] = jnp.where(rowd >= QK, qs, zero)
        qx_ref[0, HEAD:2 * HEAD, cs] = auxq
        qx_ref[1, HEAD:2 * HEAD, cs] = auxq
    lane_k = lax.broadcasted_iota(jnp.int32, (tk, HEAD), 1)
    row_k = lax.broadcasted_iota(jnp.int32, (tk, HEAD), 0)
    auxk = jnp.where(lane_k <= 1, row_k, 0).astype(F32).astype(BF16)

    lv = lam_ref[...]
    lam = (jnp.exp(jnp.sum(lv[0:1, :] * lv[1:2, :], keepdims=True))
           - jnp.exp(jnp.sum(lv[2:3, :] * lv[3:4, :], keepdims=True)) + LAMBDA_INIT)

    def phase_units(p):
        if not 0 <= p <= ncb:
            return []
        return [(s, s, p) if p <= s else (s, ncb - 1 - s, p - s - 1) for s in range(nslot)]

    m_run, alpha = {}, {}

    def finish(cb):
        cs = slice(cb * tk, (cb + 1) * tk)
        ot = (acc_ref[0, 0:HEAD, cs] / acc_ref[0, HEAD:HEAD + 1, cs]
              - lam * (acc_ref[1, 0:HEAD, cs] / acc_ref[1, HEAD:HEAD + 1, cs]))
        o = ot.T
        ms = jnp.mean(o * o, axis=-1, keepdims=True)
        y = o * lax.rsqrt(ms + EPS) * sub_ref[...] * (1.0 - LAMBDA_INIT)
        g = g_ref[cs, :]
        o_ref[cs, :] = (y * (g * jax.nn.sigmoid(g))).astype(o_ref.dtype)

    for p in range(-1, ncb + 2):
        ua, ub, uc = phase_units(p + 1), phase_units(p), phase_units(p - 1)
        for i in range(nslot):
            for mp in range(2):
                if ua:
                    s, cb, d = ua[i]
                    kx = jnp.concatenate([k_ref[d * tk:(d + 1) * tk, :], auxk], axis=1)
                    sa[(p + 1) % 2][mp, :, s * tk:(s + 1) * tk] = _dot(
                        kx, qx_ref[mp, :, cb * tk:(cb + 1) * tk], NN)
                if ub:
                    s, cb, d = ub[i]
                    ss = slice(s * tk, (s + 1) * tk)
                    st = sa[p % 2][mp, :, ss]
                    if d == cb:
                        st = st + dg_ref[...]
                    mx = jnp.max(st, axis=0, keepdims=True) + slope * float((d - cb) * tk)
                    if d == 0:
                        m_new = mx
                    else:
                        m_new = jnp.maximum(m_run[mp, cb], mx)
                        alpha[mp, cb, d] = jnp.exp2(m_run[mp, cb] - m_new)
                    m_run[mp, cb] = m_new
                    shift = m_new - slope * float((d - cb) * tk)
                    pb[p % 2][mp, :, ss] = jnp.exp2(st - shift).astype(BF16)
                if uc:
                    s, cb, d = uc[i]
                    cs = slice(cb * tk, (cb + 1) * tk)
                    pv = _dot(vt_ref[0, 0, :, d * tk:(d + 1) * tk],
                              pb[(p - 1) % 2][mp, :, s * tk:(s + 1) * tk], NN)
                    acc_ref[mp, :, cs] = pv if d == 0 else alpha[mp, cb, d] * acc_ref[mp, :, cs] + pv
            if uc and uc[i][2] == uc[i][1]:
                finish(uc[i][1])


def _attn(qt, kn, vt, proj, b_lambda, b_subln, batch, seq, tk=256):
    ncb = seq // tk
    assert seq % (2 * tk) == 0
    m = kn.shape[0]
    return pl.pallas_call(
        functools.partial(_attn_kernel, seq=seq, tk=tk),
        grid=(batch, HEADS),
        in_specs=[
            pl.BlockSpec((4, QK), lambda b, h: (0, 0)),
            pl.BlockSpec((1, HEAD), lambda b, h: (0, 0)),
            pl.BlockSpec((1, 1, HEAD, seq), lambda b, h: (b, h, 0, 0)),
            pl.BlockSpec((seq, HEAD), lambda b, h: (b, h)),
            pl.BlockSpec((1, 1, VT_ROWS, seq), lambda b, h: (b, h, 0, 0)),
            pl.BlockSpec((seq, HEAD), lambda b, h: (b, SEG_BG + h)),
        ],
        out_specs=pl.BlockSpec((seq, HEAD), lambda b, h: (b, h)),
        out_shape=jax.ShapeDtypeStruct((m, WIDTH), BF16),
        scratch_shapes=[
            pltpu.VMEM((2, tk, ncb // 2 * tk), F32), pltpu.VMEM((2, tk, ncb // 2 * tk), F32),
            pltpu.VMEM((2, tk, ncb // 2 * tk), BF16), pltpu.VMEM((2, tk, ncb // 2 * tk), BF16),
            pltpu.VMEM((2, VT_ROWS, seq), F32), pltpu.VMEM((2, 2 * HEAD, seq), BF16),
            pltpu.VMEM((tk, tk), F32),
        ],
        compiler_params=pltpu.CompilerParams(
            dimension_semantics=("arbitrary", "arbitrary"), vmem_limit_bytes=VMEM_LIMIT),
        name="attn",
    )(b_lambda, b_subln, qt, kn, vt, proj)


def _out_proj_kernel(ya_ref, yb_ref, ga_ref, gb_ref, x_ref, wa_ref, wb_ref, wo_ref, o_ref):
    ya = jnp.dot(ya_ref[...], wa_ref[...], preferred_element_type=F32)
    yb = jnp.dot(yb_ref[...], wb_ref[...], preferred_element_type=F32)
    mixed = jax.nn.sigmoid(ga_ref[...]) * ya + jax.nn.sigmoid(gb_ref[...]) * yb
    o_ref[...] = x_ref[...] + jnp.dot(mixed.astype(BF16), wo_ref[...],
                                      preferred_element_type=F32)


def _out_proj(ya, yb, proj, x2, wa, wb, wo, tm=256):
    m, d = x2.shape
    w = ya.shape[1]
    resident = functools.partial(pl.BlockSpec, pipeline_mode=pl.Buffered(1))
    return pl.pallas_call(
        _out_proj_kernel,
        grid=(m // tm,),
        in_specs=[
            pl.BlockSpec((tm, w), lambda i: (i, 0)),
            pl.BlockSpec((tm, w), lambda i: (i, 0)),
            pl.BlockSpec((tm, d), lambda i: (i, SEG_GATE_A)),
            pl.BlockSpec((tm, d), lambda i: (i, SEG_GATE_B)),
            pl.BlockSpec((tm, d), lambda i: (i, 0)),
            resident((w, d), lambda i: (0, 0)),
            resident((w, d), lambda i: (0, 0)),
            resident((d, d), lambda i: (0, 0)),
        ],
        out_specs=pl.BlockSpec((tm, d), lambda i: (i, 0)),
        out_shape=jax.ShapeDtypeStruct((m, d), F32),
        compiler_params=pltpu.CompilerParams(
            dimension_semantics=("arbitrary",), vmem_limit_bytes=VMEM_LIMIT),
        name="out_proj",
    )(ya, yb, proj, proj, x2, wa, wb, wo)


def kernel(x, norm_w, w_in, a_lower_bound, a_out_norm, b_q_norm, b_k_norm, b_lambda,
           b_subln, w_branch_a, w_branch_b, w_out):
    batch, seq, d = x.shape
    assert norm_w.shape[0] == 1 and a_lower_bound.shape == (2, WIDTH)
    assert d == 2 * WIDTH and seq % 512 == 0
    x2 = x.reshape(batch * seq, d)

    proj = _in_proj(x2, norm_w[0:1], w_in[0].astype(BF16))
    ya = _hgrn(proj, a_lower_bound, a_out_norm[0:1], batch, seq)
    qt, kn, vt = _qkv_prep(proj, jnp.tile(b_q_norm[0:1], (1, 2)),
                           jnp.tile(b_k_norm[0:1], (1, 2)), batch, seq)
    yb = _attn(qt, kn, vt, proj, b_lambda[0], b_subln[0:1], batch, seq)
    out = _out_proj(ya, yb, proj, x2, w_branch_a[0].astype(BF16),
                    w_branch_b[0].astype(BF16), w_out[0].astype(BF16))
    return out.reshape(batch, seq, d)
```

```python
import functools
import math

import jax
import jax.numpy as jnp
from jax import lax
from jax.experimental import pallas as pl
from jax.experimental.pallas import tpu as pltpu

F32 = jnp.float32
BF16 = jnp.bfloat16

EPS = 1e-6
CHUNK = 64
HEAD = 128
HEADS = 8
QK = 64
WIDTH = HEADS * HEAD
SUB = 8
LAMBDA_INIT = 0.8 - 0.6 * math.exp(-0.3 * 0)
NEG = -1e30
LOG2E = math.log2(math.e)
VT_ROWS = HEAD + 16

SEG_AQ, SEG_AF, SEG_AI, SEG_AG, SEG_BQ, SEG_BK, SEG_BV, SEG_BG = (s * HEADS for s in range(8))
SEG_GATE_A, SEG_GATE_B = 4, 5

VMEM_LIMIT = 56 * 1024 * 1024


def _dot(a, b, dims):
    return lax.dot_general(a, b, (dims, ((), ())), preferred_element_type=F32)


NN = ((1,), (0,))
NT = ((1,), (1,))
TN = ((0,), (0,))


def _in_proj_kernel(x_ref, nw_ref, w_ref, o_ref, h_ref):
    @pl.when(pl.program_id(1) == 0)
    def _():
        x = x_ref[...]
        ms = jnp.mean(x * x, axis=-1, keepdims=True)
        h_ref[...] = (x * lax.rsqrt(ms + EPS) * nw_ref[...]).astype(BF16)

    o_ref[...] = jnp.dot(h_ref[...], w_ref[...], preferred_element_type=F32)


def _in_proj(x2, norm_w, w_in_bf16, tm=1024, tn=1024):
    m, d = x2.shape
    n = w_in_bf16.shape[1]
    return pl.pallas_call(
        _in_proj_kernel,
        grid=(m // tm, n // tn),
        in_specs=[
            pl.BlockSpec((tm, d), lambda i, j: (i, 0)),
            pl.BlockSpec((1, d), lambda i, j: (0, 0)),
            pl.BlockSpec((d, tn), lambda i, j: (0, j)),
        ],
        out_specs=pl.BlockSpec((tm, tn), lambda i, j: (i, j)),
        out_shape=jax.ShapeDtypeStruct((m, n), F32),
        scratch_shapes=[pltpu.VMEM((tm, d), BF16)],
        compiler_params=pltpu.CompilerParams(
            dimension_semantics=("arbitrary", "arbitrary"), vmem_limit_bytes=VMEM_LIMIT),
        name="in_proj",
    )(x2, norm_w, w_in_bf16)


def _hgrn_kernel(alb_ref, gain_ref, aq_ref, af_ref, ai_ref, ag_ref, o_ref,
                 st_ref, qe_s, u_s, dec_s, intra_s, gs_s, *, tt, nt):
    t = pl.program_id(2)
    nch = tt // CHUNK
    gain = gain_ref[...]

    @pl.when((pl.program_id(0) == 0) & (pl.program_id(1) == 0) & (t == 0))
    def _():
        for ref in (st_ref, qe_s, u_s, dec_s, intra_s, gs_s):
            ref[...] = jnp.zeros_like(ref)

    def rows(x, c):
        return x[c * CHUNK:(c + 1) * CHUNK]

    def back():
        st = st_ref[...]
        outs = []
        for c in range(nch):
            outs.append(intra_s[c * CHUNK:(c + 1) * CHUNK, :]
                        + _dot(qe_s[c * CHUNK:(c + 1) * CHUNK, :], st.astype(BF16), NT))
            st = st * dec_s[c:c + 1, :] + u_s[c]
        out = jnp.concatenate(outs, axis=0)
        ms = jnp.mean(out * out, axis=-1, keepdims=True)
        o_ref[...] = (out * lax.rsqrt(ms + EPS) * gain * gs_s[...]).astype(o_ref.dtype)
        return st

    def front():
        a = alb_ref[...]
        a0, a1 = a[0:1, :], a[1:2, :]
        amax = jnp.maximum(a0, a1)
        e0, e1 = jnp.exp(a0 - amax), jnp.exp(a1 - amax)
        lb = e0 / (e0 + e1)
        one_m_lb = 1.0 - lb

        row = lax.broadcasted_iota(jnp.int32, (CHUNK, CHUNK), 0)
        col = lax.broadcasted_iota(jnp.int32, (CHUNK, CHUNK), 1)
        tril = (row >= col).astype(BF16)
        same32 = (row >> 5) == (col >> 5)
        same16 = (row >> 4) == (col >> 4)
        rowt = lax.broadcasted_iota(jnp.int32, (tt, HEAD), 0)
        nb = CHUNK // SUB
        sub4 = lax.broadcasted_iota(jnp.int32, (1, 1, SUB, HEAD), 2)
        rel4 = (lax.broadcasted_iota(jnp.int32, (1, nb, SUB, CHUNK), 3)
                - SUB * lax.broadcasted_iota(jnp.int32, (1, nb, SUB, CHUNK), 1))

        def row_bcast(x, step, off):
            return jnp.concatenate(
                [jnp.broadcast_to(x[j * step + off:j * step + off + 1, :], (step, HEAD))
                 for j in range(tt // step)], axis=0)

        z = af_ref[...]
        qa = aq_ref[...]
        v = ai_ref[...].astype(BF16)
        g = ag_ref[...]
        sig = jax.nn.sigmoid(z)
        f = lb + one_m_lb * sig
        lf = jnp.log(f)
        k = one_m_lb * (1.0 - sig)
        q = qa * jax.nn.sigmoid(qa)

        hi = lf.astype(BF16)
        r1 = lf - hi.astype(F32)
        md = r1.astype(BF16)
        lo = (r1 - md.astype(F32)).astype(BF16)
        b = jnp.concatenate(
            [_dot(tril, rows(hi, c), NN) + _dot(tril, rows(md, c), NN)
             + _dot(tril, rows(lo, c), NN) for c in range(nch)], axis=0)
        b_last = row_bcast(b, CHUNK, CHUNK - 1)
        qe = (q * jnp.exp(b)).astype(BF16)
        kd = (k * jnp.exp(b_last - b)).astype(BF16)
        dec = jnp.exp(b_last)

        def level(blk):
            half = blk // 2
            upper = (rowt & (blk - 1)) >= half
            d = b - row_bcast(b, blk, half)
            e = jnp.exp(jnp.where(upper, d, -d))
            ql = jnp.where(upper, q * e, 0.0).astype(BF16)
            kl = jnp.where(upper, 0.0, k * e).astype(BF16)
            return [_dot(rows(kl, c), rows(ql, c), NT) for c in range(nch)]

        lv64, lv32, lv16 = level(64), level(32), level(16)

        shape4 = (nch, nb, SUB, HEAD)
        k4, q4, f4 = k.reshape(shape4), q.reshape(shape4), f.reshape(shape4)
        gk = jnp.zeros(shape4, F32)
        pt = jnp.zeros((nch, nb, SUB, CHUNK), F32)
        for tl in range(SUB):
            fb = jnp.broadcast_to(f4[:, :, tl:tl + 1, :], shape4)
            qb = jnp.broadcast_to(q4[:, :, tl:tl + 1, :], shape4)
            gk = jnp.where(sub4 == tl, k4, gk * fb)
            p = jnp.sum(gk * qb, axis=3, keepdims=True)
            pt = jnp.where(rel4 == tl, p, pt)
        pt = pt.reshape(tt, CHUNK)

        intras, updates = [], []
        for c in range(nch):
            sc = (lv64[c] + jnp.where(same32, lv32[c], 0.0) + jnp.where(same16, lv16[c], 0.0)
                  + rows(pt, c))
            intras.append(_dot(sc.astype(BF16), rows(v, c), TN))
            updates.append(_dot(rows(v, c), rows(kd, c), TN))
        decs = jnp.concatenate([dec[c * CHUNK:c * CHUNK + 1] for c in range(nch)], axis=0)
        return qe, updates, decs, jnp.concatenate(intras, axis=0), g * jax.nn.sigmoid(g)

    @pl.when(t < nt)
    def _():
        st = back()
        qe, updates, decs, intra, gs = front()
        st_ref[...] = jnp.where(t == 0, 0.0, st)
        qe_s[...] = qe
        for c in range(nch):
            u_s[c] = updates[c]
        dec_s[...] = decs
        intra_s[...] = intra
        gs_s[...] = gs

    @pl.when(t == nt)
    def _():
        back()


def _hgrn(proj, a_lower_bound, a_out_norm, batch, seq, tt=512):
    m = proj.shape[0]
    nt = seq // tt
    nch = tt // CHUNK

    def seg(s):
        return pl.BlockSpec((tt, HEAD),
                            lambda b, h, t, s=s: (b * nt + jnp.minimum(t, nt - 1), s + h))

    return pl.pallas_call(
        functools.partial(_hgrn_kernel, tt=tt, nt=nt),
        grid=(batch, HEADS, nt + 1),
        in_specs=[
            pl.BlockSpec((2, HEAD), lambda b, h, t: (0, h)),
            pl.BlockSpec((1, HEAD), lambda b, h, t: (0, 0)),
            seg(SEG_AQ), seg(SEG_AF), seg(SEG_AI), seg(SEG_AG),
        ],
        out_specs=pl.BlockSpec((tt, HEAD), lambda b, h, t: (b * nt + jnp.maximum(t - 1, 0), h)),
        out_shape=jax.ShapeDtypeStruct((m, WIDTH), BF16),
        scratch_shapes=[
            pltpu.VMEM((HEAD, HEAD), F32),
            pltpu.VMEM((tt, HEAD), BF16),
            pltpu.VMEM((nch, HEAD, HEAD), F32),
            pltpu.VMEM((nch, HEAD), F32),
            pltpu.VMEM((tt, HEAD), F32),
            pltpu.VMEM((tt, HEAD), F32),
        ],
        compiler_params=pltpu.CompilerParams(
            dimension_semantics=("arbitrary", "arbitrary", "arbitrary"),
            vmem_limit_bytes=VMEM_LIMIT),
        name="hgrn",
    )(a_lower_bound, a_out_norm, proj, proj, proj, proj)


def _qkv_prep_kernel(qg_ref, kg_ref, bq_ref, bk_ref, bv_ref, qt_ref, kn_ref, vt_ref):
    r = lax.broadcasted_iota(jnp.int32, (HEAD, HEAD), 0)
    c = lax.broadcasted_iota(jnp.int32, (HEAD, HEAD), 1)
    grp = ((r >> 6) == (c >> 6)).astype(BF16)

    def gnorm(x, gain):
        x2 = x * x
        hi = x2.astype(BF16)
        lo = (x2 - hi.astype(F32)).astype(BF16)
        ss = _dot(hi, grp, NN) + _dot(lo, grp, NN)
        return x * lax.rsqrt(ss * (1.0 / QK) + EPS) * gain

    tr = bq_ref.shape[0]
    pad_row = lax.broadcasted_iota(jnp.int32, (VT_ROWS - HEAD, tr), 0)
    ones_row = (pad_row == 0).astype(BF16)
    for h in range(HEADS):
        hs = slice(h * HEAD, (h + 1) * HEAD)
        qn = gnorm(bq_ref[:, hs], qg_ref[...]) * (QK ** -0.5 * LOG2E)
        qt_ref[0, h] = qn.T.astype(BF16)
        kn_ref[:, hs] = gnorm(bk_ref[:, hs], kg_ref[...]).astype(BF16)
        vt_ref[0, h, 0:HEAD, :] = bv_ref[:, hs].T.astype(BF16)
        vt_ref[0, h, HEAD:VT_ROWS, :] = ones_row


def _qkv_prep(proj, q_gain, k_gain, batch, seq, tr=512):
    m = proj.shape[0]
    nt = seq // tr

    def seg(s):
        return pl.BlockSpec((tr, WIDTH), lambda b, t, s=s: (b * nt + t, s // HEADS))

    def col_out(rows):
        return pl.BlockSpec((1, HEADS, rows, tr), lambda b, t: (b, 0, 0, t))

    return pl.pallas_call(
        _qkv_prep_kernel,
        grid=(batch, nt),
        in_specs=[
            pl.BlockSpec((1, HEAD), lambda b, t: (0, 0)),
            pl.BlockSpec((1, HEAD), lambda b, t: (0, 0)),
            seg(SEG_BQ), seg(SEG_BK), seg(SEG_BV),
        ],
        out_specs=[col_out(HEAD), pl.BlockSpec((tr, WIDTH), lambda b, t: (b * nt + t, 0)),
                   col_out(VT_ROWS)],
        out_shape=[jax.ShapeDtypeStruct((batch, HEADS, HEAD, seq), BF16),
                   jax.ShapeDtypeStruct((m, WIDTH), BF16),
                   jax.ShapeDtypeStruct((batch, HEADS, VT_ROWS, seq), BF16)],
        compiler_params=pltpu.CompilerParams(
            dimension_semantics=("arbitrary", "arbitrary"), vmem_limit_bytes=VMEM_LIMIT),
        name="qkv_prep",
    )(q_gain, k_gain, proj, proj, proj)


def _attn_kernel(lam_ref, sub_ref, qt_ref, k_ref, vt_ref, g_ref, o_ref,
                 sa0, sa1, pb0, pb1, acc_ref, qx_ref, dg_ref, *, seq, tk):
    sa = (sa0, sa1)
    pb = (pb0, pb1)
    ncb = seq // tk
    nslot = ncb // 2
    h = pl.program_id(1)
    slope = jnp.exp2(-(jnp.zeros((1, 1), F32) + (h + 1).astype(F32))) * LOG2E
    c_hi = slope.astype(BF16).astype(F32)
    c_lo = (slope - c_hi).astype(BF16).astype(F32)
    slope = c_hi + c_lo

    s_i = lax.broadcasted_iota(jnp.int32, (tk, tk), 0)
    t_i = lax.broadcasted_iota(jnp.int32, (tk, tk), 1)
    corr = slope * jnp.where(s_i <= t_i, 0, 2 * (t_i - s_i)).astype(F32)
    dg_ref[...] = jnp.where((s_i >> 6) <= (t_i >> 6), corr, NEG)

    rowd = lax.broadcasted_iota(jnp.int32, (HEAD, tk), 0)
    auxq = jnp.where(rowd == 0, c_hi, jnp.where(rowd == 1, c_lo, 0.0)).astype(BF16)
    for cb in range(ncb):
        cs = slice(cb * tk, (cb + 1) * tk)
        qs = qt_ref[0, 0, :, cs]
        zero = jnp.zeros_like(qs)
        qx_ref[0, 0:HEAD, cs] = jnp.where(rowd < QK, qs, zero)
        qx_ref[1, 0:HEAD, cs] = jnp.where(rowd >= QK, qs, zero)
        qx_ref[0, HEAD:2 * HEAD, cs] = auxq
        qx_ref[1, HEAD:2 * HEAD, cs] = auxq
    lane_k = lax.broadcasted_iota(jnp.int32, (tk, HEAD), 1)
    row_k = lax.broadcasted_iota(jnp.int32, (tk, HEAD), 0)
    auxk = jnp.where(lane_k <= 1, row_k, 0).astype(F32).astype(BF16)

    lv = lam_ref[...]
    lam = (jnp.exp(jnp.sum(lv[0:1, :] * lv[1:2, :], keepdims=True))
           - jnp.exp(jnp.sum(lv[2:3, :] * lv[3:4, :], keepdims=True)) + LAMBDA_INIT)

    def phase_units(p):
        if not 0 <= p <= ncb:
            return []
        return [(s, s, p) if p <= s else (s, ncb - 1 - s, p - s - 1) for s in range(nslot)]

    m_run, alpha = {}, {}

    def finish(cb):
        cs = slice(cb * tk, (cb + 1) * tk)
        ot = (acc_ref[0, 0:HEAD, cs] / acc_ref[0, HEAD:HEAD + 1, cs]
              - lam * (acc_ref[1, 0:HEAD, cs] / acc_ref[1, HEAD:HEAD + 1, cs]))
        o = ot.T
        ms = jnp.mean(o * o, axis=-1, keepdims=True)
        y = o * lax.rsqrt(ms + EPS) * sub_ref[...] * (1.0 - LAMBDA_INIT)
        g = g_ref[cs, :]
        o_ref[cs, :] = (y * (g * jax.nn.sigmoid(g))).astype(o_ref.dtype)

    for p in range(-1, ncb + 2):
        ua, ub, uc = phase_units(p + 1), phase_units(p), phase_units(p - 1)
        for i in range(nslot):
            for mp in range(2):
                if ua:
                    s, cb, d = ua[i]
                    kx = jnp.concatenate([k_ref[d * tk:(d + 1) * tk, :], auxk], axis=1)
                    sa[(p + 1) % 2][mp, :, s * tk:(s + 1) * tk] = _dot(
                        kx, qx_ref[mp, :, cb * tk:(cb + 1) * tk], NN)
                if ub:
                    s, cb, d = ub[i]
                    ss = slice(s * tk, (s + 1) * tk)
                    st = sa[p % 2][mp, :, ss]
                    if d == cb:
                        st = st + dg_ref[...]
                    mx = jnp.max(st, axis=0, keepdims=True) + slope * float((d - cb) * tk)
                    if d == 0:
                        m_new = mx
                    else:
                        m_new = jnp.maximum(m_run[mp, cb], mx)
                        alpha[mp, cb, d] = jnp.exp2(m_run[mp, cb] - m_new)
                    m_run[mp, cb] = m_new
                    shift = m_new - slope * float((d - cb) * tk)
                    pb[p % 2][mp, :, ss] = jnp.exp2(st - shift).astype(BF16)
                if uc:
                    s, cb, d = uc[i]
                    cs = slice(cb * tk, (cb + 1) * tk)
                    pv = _dot(vt_ref[0, 0, :, d * tk:(d + 1) * tk],
                              pb[(p - 1) % 2][mp, :, s * tk:(s + 1) * tk], NN)
                    acc_ref[mp, :, cs] = pv if d == 0 else alpha[mp, cb, d] * acc_ref[mp, :, cs] + pv
            if uc and uc[i][2] == uc[i][1]:
                finish(uc[i][1])


def _attn(qt, kn, vt, proj, b_lambda, b_subln, batch, seq, tk=256):
    ncb = seq // tk
    assert seq % (2 * tk) == 0
    m = kn.shape[0]
    return pl.pallas_call(
        functools.partial(_attn_kernel, seq=seq, tk=tk),
        grid=(batch, HEADS),
        in_specs=[
            pl.BlockSpec((4, QK), lambda b, h: (0, 0)),
            pl.BlockSpec((1, HEAD), lambda b, h: (0, 0)),
            pl.BlockSpec((1, 1, HEAD, seq), lambda b, h: (b, h, 0, 0)),
            pl.BlockSpec((seq, HEAD), lambda b, h: (b, h)),
            pl.BlockSpec((1, 1, VT_ROWS, seq), lambda b, h: (b, h, 0, 0)),
            pl.BlockSpec((seq, HEAD), lambda b, h: (b, SEG_BG + h)),
        ],
        out_specs=pl.BlockSpec((seq, HEAD), lambda b, h: (b, h)),
        out_shape=jax.ShapeDtypeStruct((m, WIDTH), BF16),
        scratch_shapes=[
            pltpu.VMEM((2, tk, ncb // 2 * tk), F32), pltpu.VMEM((2, tk, ncb // 2 * tk), F32),
            pltpu.VMEM((2, tk, ncb // 2 * tk), BF16), pltpu.VMEM((2, tk, ncb // 2 * tk), BF16),
            pltpu.VMEM((2, VT_ROWS, seq), F32), pltpu.VMEM((2, 2 * HEAD, seq), BF16),
            pltpu.VMEM((tk, tk), F32),
        ],
        compiler_params=pltpu.CompilerParams(
            dimension_semantics=("arbitrary", "arbitrary"), vmem_limit_bytes=VMEM_LIMIT),
        name="attn",
    )(b_lambda, b_subln, qt, kn, vt, proj)


def _out_proj_kernel(ya_ref, yb_ref, ga_ref, gb_ref, x_ref, wa_ref, wb_ref, wo_ref, o_ref):
    ya = jnp.dot(ya_ref[...], wa_ref[...], preferred_element_type=F32)
    yb = jnp.dot(yb_ref[...], wb_ref[...], preferred_element_type=F32)
    mixed = jax.nn.sigmoid(ga_ref[...]) * ya + jax.nn.sigmoid(gb_ref[...]) * yb
    o_ref[...] = x_ref[...] + jnp.dot(mixed.astype(BF16), wo_ref[...],
                                      preferred_element_type=F32)


def _out_proj(ya, yb, proj, x2, wa, wb, wo, tm=256):
    m, d = x2.shape
    w = ya.shape[1]
    resident = functools.partial(pl.BlockSpec, pipeline_mode=pl.Buffered(1))
    return pl.pallas_call(
        _out_proj_kernel,
        grid=(m // tm,),
        in_specs=[
            pl.BlockSpec((tm, w), lambda i: (i, 0)),
            pl.BlockSpec((tm, w), lambda i: (i, 0)),
            pl.BlockSpec((tm, d), lambda i: (i, SEG_GATE_A)),
            pl.BlockSpec((tm, d), lambda i: (i, SEG_GATE_B)),
            pl.BlockSpec((tm, d), lambda i: (i, 0)),
            resident((w, d), lambda i: (0, 0)),
            resident((w, d), lambda i: (0, 0)),
            resident((d, d), lambda i: (0, 0)),
        ],
        out_specs=pl.BlockSpec((tm, d), lambda i: (i, 0)),
        out_shape=jax.ShapeDtypeStruct((m, d), F32),
        compiler_params=pltpu.CompilerParams(
            dimension_semantics=("arbitrary",), vmem_limit_bytes=VMEM_LIMIT),
        name="out_proj",
    )(ya, yb, proj, proj, x2, wa, wb, wo)


def kernel(x, norm_w, w_in, a_lower_bound, a_out_norm, b_q_norm, b_k_norm, b_lambda,
           b_subln, w_branch_a, w_branch_b, w_out):
    batch, seq, d = x.shape
    assert norm_w.shape[0] == 1 and a_lower_bound.shape == (2, WIDTH)
    assert d == 2 * WIDTH and seq % 512 == 0
    x2 = x.reshape(batch * seq, d)

    proj = _in_proj(x2, norm_w[0:1], w_in[0].astype(BF16))
    ya = _hgrn(proj, a_lower_bound, a_out_norm[0:1], batch, seq)
    qt, kn, vt = _qkv_prep(proj, jnp.tile(b_q_norm[0:1], (1, 2)),
                           jnp.tile(b_k_norm[0:1], (1, 2)), batch, seq)
    yb = _attn(qt, kn, vt, proj, b_lambda[0], b_subln[0:1], batch, seq)
    out = _out_proj(ya, yb, proj, x2, w_branch_a[0].astype(BF16),
                    w_branch_b[0].astype(BF16), w_out[0].astype(BF16))
    return out.reshape(batch, seq, d)
```

```python
import functools
import math

import jax
import jax.numpy as jnp
from jax import lax
from jax.experimental import pallas as pl
from jax.experimental.pallas import tpu as pltpu

F32 = jnp.float32
BF16 = jnp.bfloat16

EPS = 1e-6
CHUNK = 64
HEAD = 128
HEADS = 8
QK = 64
WIDTH = HEADS * HEAD
SUB = 8
LAMBDA_INIT = 0.8 - 0.6 * math.exp(-0.3 * 0)
NEG = -1e30
LOG2E = math.log2(math.e)
VT_ROWS = HEAD + 16

SEG_AQ, SEG_AF, SEG_AI, SEG_AG, SEG_BQ, SEG_BK, SEG_BV, SEG_BG = (s * HEADS for s in range(8))
SEG_GATE_A, SEG_GATE_B = 4, 5

VMEM_LIMIT = 56 * 1024 * 1024


def _dot(a, b, dims):
    return lax.dot_general(a, b, (dims, ((), ())), preferred_element_type=F32)


NN = ((1,), (0,))
NT = ((1,), (1,))
TN = ((0,), (0,))


def _in_proj_kernel(x_ref, nw_ref, w_ref, o_ref, h_ref):
    @pl.when(pl.program_id(1) == 0)
    def _():
        x = x_ref[...]
        ms = jnp.mean(x * x, axis=-1, keepdims=True)
        h_ref[...] = (x * lax.rsqrt(ms + EPS) * nw_ref[...]).astype(BF16)

    o_ref[...] = jnp.dot(h_ref[...], w_ref[...], preferred_element_type=F32)


def _in_proj(x2, norm_w, w_in_bf16, tm=1024, tn=1024):
    m, d = x2.shape
    n = w_in_bf16.shape[1]
    return pl.pallas_call(
        _in_proj_kernel,
        grid=(m // tm, n // tn),
        in_specs=[
            pl.BlockSpec((tm, d), lambda i, j: (i, 0)),
            pl.BlockSpec((1, d), lambda i, j: (0, 0)),
            pl.BlockSpec((d, tn), lambda i, j: (0, j)),
        ],
        out_specs=pl.BlockSpec((tm, tn), lambda i, j: (i, j)),
        out_shape=jax.ShapeDtypeStruct((m, n), F32),
        scratch_shapes=[pltpu.VMEM((tm, d), BF16)],
        compiler_params=pltpu.CompilerParams(
            dimension_semantics=("arbitrary", "arbitrary"), vmem_limit_bytes=VMEM_LIMIT),
        name="in_proj",
    )(x2, norm_w, w_in_bf16)


def _hgrn_kernel(alb_ref, gain_ref, aq_ref, af_ref, ai_ref, ag_ref, o_ref, st_ref, *, tt):
    @pl.when(pl.program_id(2) == 0)
    def _():
        st_ref[...] = jnp.zeros_like(st_ref)

    a = alb_ref[...]
    a0, a1 = a[0:1, :], a[1:2, :]
    amax = jnp.maximum(a0, a1)
    e0, e1 = jnp.exp(a0 - amax), jnp.exp(a1 - amax)
    lb = e0 / (e0 + e1)
    one_m_lb = 1.0 - lb
    gain = gain_ref[...]

    nch = tt // CHUNK
    nb = CHUNK // SUB
    row = lax.broadcasted_iota(jnp.int32, (CHUNK, CHUNK), 0)
    col = lax.broadcasted_iota(jnp.int32, (CHUNK, CHUNK), 1)
    tril = (row >= col).astype(BF16)

    def pair_mask(blk):
        return (((row & (blk - 1)) >= blk // 2) & ((col & (blk - 1)) < blk // 2)
                & ((row & -blk) == (col & -blk)))

    rowt = lax.broadcasted_iota(jnp.int32, (tt, HEAD), 0)
    tl4 = lax.broadcasted_iota(jnp.int32, (1, nb, SUB, CHUNK), 2)
    rel4 = (lax.broadcasted_iota(jnp.int32, (1, nb, SUB, CHUNK), 3)
            - SUB * lax.broadcasted_iota(jnp.int32, (1, nb, SUB, CHUNK), 1))

    def rows(x, c):
        return x[c * CHUNK:(c + 1) * CHUNK]

    def row_bcast(x, step, off):
        return jnp.concatenate(
            [jnp.broadcast_to(x[j * step + off:j * step + off + 1, :], (step, HEAD))
             for j in range(tt // step)], axis=0)

    z = af_ref[...]
    qa = aq_ref[...]
    v = ai_ref[...].astype(BF16)
    g = ag_ref[...]
    f = lb + one_m_lb * jax.nn.sigmoid(z)
    lf = jnp.log(f)
    k = 1.0 - f
    q = qa * jax.nn.sigmoid(qa)

    hi = lf.astype(BF16)
    lo = (lf - hi.astype(F32)).astype(BF16)
    b = jnp.concatenate(
        [_dot(tril, rows(hi, c), NN) + _dot(tril, rows(lo, c), NN) for c in range(nch)], axis=0)
    b_last = row_bcast(b, CHUNK, CHUNK - 1)
    qe = (q * jnp.exp(b)).astype(BF16)
    kd = (k * jnp.exp(b_last - b)).astype(BF16)
    dec = jnp.exp(b_last)

    def level(blk):
        half = blk // 2
        upper = (rowt & (blk - 1)) >= half
        d = b - row_bcast(b, blk, half)
        e = jnp.exp(jnp.where(upper, d, -d))
        xe = (jnp.where(upper, q, k) * e).astype(BF16)
        return [_dot(rows(xe, c), rows(xe, c), NT) for c in range(nch)]

    lv64, lv32, lv16 = level(64), level(32), level(16)
    m64, m32, m16 = pair_mask(64), pair_mask(32), pair_mask(16)

    shape4 = (nch, nb, SUB, HEAD)
    q4, f4 = q.reshape(shape4), f.reshape(shape4)
    gk = k.reshape(shape4)
    pt = jnp.zeros((nch, nb, SUB, CHUNK), F32)
    for j in range(SUB):
        if j:
            gk = f4 * pltpu.roll(gk, 1, axis=2)
        p = jnp.sum(q4 * gk, axis=3, keepdims=True)
        pt = jnp.where((rel4 == tl4 - j) & (tl4 >= j), p, pt)
    pt = pt.reshape(tt, CHUNK)

    intras = []
    for c in range(nch):
        sc = (jnp.where(m64, lv64[c], 0.0) + jnp.where(m32, lv32[c], 0.0)
              + jnp.where(m16, lv16[c], 0.0) + rows(pt, c))
        intras.append(_dot(sc.astype(BF16), rows(v, c), NN))
    updates = [_dot(rows(v, c), rows(kd, c), TN) for c in range(nch)]
    st = st_ref[...]
    outs = []
    for c in range(nch):
        outs.append(intras[c] + _dot(rows(qe, c), st.astype(BF16), NT))
        st = st * dec[c * CHUNK:c * CHUNK + 1] + updates[c]
    st_ref[...] = st

    out = jnp.concatenate(outs, axis=0)
    ms = jnp.mean(out * out, axis=-1, keepdims=True)
    y = out * lax.rsqrt(ms + EPS) * gain
    o_ref[...] = (y * (g * jax.nn.sigmoid(g))).astype(o_ref.dtype)


def _hgrn(proj, a_lower_bound, a_out_norm, batch, seq, tt=512):
    m = proj.shape[0]
    nt = seq // tt

    def seg(s):
        return pl.BlockSpec((tt, HEAD), lambda b, h, t, s=s: (b * nt + t, s + h))

    return pl.pallas_call(
        functools.partial(_hgrn_kernel, tt=tt),
        grid=(batch, HEADS, nt),
        in_specs=[
            pl.BlockSpec((2, HEAD), lambda b, h, t: (0, h)),
            pl.BlockSpec((1, HEAD), lambda b, h, t: (0, 0)),
            seg(SEG_AQ), seg(SEG_AF), seg(SEG_AI), seg(SEG_AG),
        ],
        out_specs=pl.BlockSpec((tt, HEAD), lambda b, h, t: (b * nt + t, h)),
        out_shape=jax.ShapeDtypeStruct((m, WIDTH), BF16),
        scratch_shapes=[pltpu.VMEM((HEAD, HEAD), F32)],
        compiler_params=pltpu.CompilerParams(
            dimension_semantics=("arbitrary", "arbitrary", "arbitrary"),
            vmem_limit_bytes=VMEM_LIMIT),
        name="hgrn",
    )(a_lower_bound, a_out_norm, proj, proj, proj, proj)


def _qkv_prep_kernel(qg_ref, kg_ref, bq_ref, bk_ref, bv_ref, qt_ref, kn_ref, vt_ref):
    r = lax.broadcasted_iota(jnp.int32, (HEAD, HEAD), 0)
    c = lax.broadcasted_iota(jnp.int32, (HEAD, HEAD), 1)
    grp = ((r >> 6) == (c >> 6)).astype(BF16)

    def gnorm(x, gain):
        x2 = x * x
        hi = x2.astype(BF16)
        lo = (x2 - hi.astype(F32)).astype(BF16)
        ss = _dot(hi, grp, NN) + _dot(lo, grp, NN)
        return x * lax.rsqrt(ss * (1.0 / QK) + EPS) * gain

    tr = bq_ref.shape[0]
    pad_row = lax.broadcasted_iota(jnp.int32, (VT_ROWS - HEAD, tr), 0)
    ones_row = (pad_row == 0).astype(BF16)
    for h in range(HEADS):
        hs = slice(h * HEAD, (h + 1) * HEAD)
        qn = gnorm(bq_ref[:, hs], qg_ref[...]) * (QK ** -0.5 * LOG2E)
        qt_ref[0, h] = qn.T.astype(BF16)
        kn_ref[:, hs] = gnorm(bk_ref[:, hs], kg_ref[...]).astype(BF16)
        vt_ref[0, h, 0:HEAD, :] = bv_ref[:, hs].T.astype(BF16)
        vt_ref[0, h, HEAD:VT_ROWS, :] = ones_row


def _qkv_prep(proj, q_gain, k_gain, batch, seq, tr=512):
    m = proj.shape[0]
    nt = seq // tr

    def seg(s):
        return pl.BlockSpec((tr, WIDTH), lambda b, t, s=s: (b * nt + t, s // HEADS))

    def col_out(rows):
        return pl.BlockSpec((1, HEADS, rows, tr), lambda b, t: (b, 0, 0, t))

    return pl.pallas_call(
        _qkv_prep_kernel,
        grid=(batch, nt),
        in_specs=[
            pl.BlockSpec((1, HEAD), lambda b, t: (0, 0)),
            pl.BlockSpec((1, HEAD), lambda b, t: (0, 0)),
            seg(SEG_BQ), seg(SEG_BK), seg(SEG_BV),
        ],
        out_specs=[col_out(HEAD), pl.BlockSpec((tr, WIDTH), lambda b, t: (b * nt + t, 0)),
                   col_out(VT_ROWS)],
        out_shape=[jax.ShapeDtypeStruct((batch, HEADS, HEAD, seq), BF16),
                   jax.ShapeDtypeStruct((m, WIDTH), BF16),
                   jax.ShapeDtypeStruct((batch, HEADS, VT_ROWS, seq), BF16)],
        compiler_params=pltpu.CompilerParams(
            dimension_semantics=("arbitrary", "arbitrary"), vmem_limit_bytes=VMEM_LIMIT),
        name="qkv_prep",
    )(q_gain, k_gain, proj, proj, proj)


def _attn_kernel(lam_ref, sub_ref, qt_ref, k_ref, vt_ref, g_ref, o_ref,
                 sa0, sa1, pb0, pb1, acc_ref, qx_ref, dg_ref, *, seq, tk):
    sa = (sa0, sa1)
    pb = (pb0, pb1)
    ncb = seq // tk
    nslot = ncb // 2
    h = pl.program_id(1)
    slope = jnp.exp2(-(jnp.zeros((1, 1), F32) + (h + 1).astype(F32))) * LOG2E
    c_hi = slope.astype(BF16).astype(F32)
    c_lo = (slope - c_hi).astype(BF16).astype(F32)
    slope = c_hi + c_lo

    s_i = lax.broadcasted_iota(jnp.int32, (tk, tk), 0)
    t_i = lax.broadcasted_iota(jnp.int32, (tk, tk), 1)
    corr = slope * jnp.where(s_i <= t_i, 0, 2 * (t_i - s_i)).astype(F32)
    dg_ref[...] = jnp.where((s_i >> 6) <= (t_i >> 6), corr, NEG)

    rowd = lax.broadcasted_iota(jnp.int32, (HEAD, tk), 0)
    auxq = jnp.where(rowd == 0, c_hi, jnp.where(rowd == 1, c_lo, 0.0)).astype(BF16)
    for cb in range(ncb):
        cs = slice(cb * tk, (cb + 1) * tk)
        qs = qt_ref[0, 0, :, cs]
        zero = jnp.zeros_like(qs)
        qx_ref[0, 0:HEAD, cs] = jnp.where(rowd < QK, qs, zero)
        qx_ref[1, 0:HEAD, cs] = jnp.where(rowd >= QK, qs, zero)
        qx_ref[0, HEAD:2 * HEAD, cs] = auxq
        qx_ref[1, HEAD:2 * HEAD, cs] = auxq
    lane_k = lax.broadcasted_iota(jnp.int32, (tk, HEAD), 1)
    row_k = lax.broadcasted_iota(jnp.int32, (tk, HEAD), 0)
    auxk = jnp.where(lane_k <= 1, row_k, 0).astype(F32).astype(BF16)

    lv = lam_ref[...]
    lam = (jnp.exp(jnp.sum(lv[0:1, :] * lv[1:2, :], keepdims=True))
           - jnp.exp(jnp.sum(lv[2:3, :] * lv[3:4, :], keepdims=True)) + LAMBDA_INIT)

    def phase_units(p):
        if not 0 <= p <= ncb:
            return []
        return [(s, s, p) if p <= s else (s, ncb - 1 - s, p - s - 1) for s in range(nslot)]

    m_run, alpha = {}, {}

    def finish(cb):
        cs = slice(cb * tk, (cb + 1) * tk)
        ot = (acc_ref[0, 0:HEAD, cs] / acc_ref[0, HEAD:HEAD + 1, cs]
              - lam * (acc_ref[1, 0:HEAD, cs] / acc_ref[1, HEAD:HEAD + 1, cs]))
        o = ot.T
        ms = jnp.mean(o * o, axis=-1, keepdims=True)
        y = o * lax.rsqrt(ms + EPS) * sub_ref[...] * (1.0 - LAMBDA_INIT)
        g = g_ref[cs, :]
        o_ref[cs, :] = (y * (g * jax.nn.sigmoid(g))).astype(o_ref.dtype)

    for p in range(-1, ncb + 2):
        ua, ub, uc = phase_units(p + 1), phase_units(p), phase_units(p - 1)
        for i in range(nslot):
            for mp in range(2):
                if ua:
                    s, cb, d = ua[i]
                    kx = jnp.concatenate([k_ref[d * tk:(d + 1) * tk, :], auxk], axis=1)
                    sa[(p + 1) % 2][mp, :, s * tk:(s + 1) * tk] = _dot(
                        kx, qx_ref[mp, :, cb * tk:(cb + 1) * tk], NN)
                if ub:
                    s, cb, d = ub[i]
                    ss = slice(s * tk, (s + 1) * tk)
                    st = sa[p % 2][mp, :, ss]
                    if d == cb:
                        st = st + dg_ref[...]
                    mx = jnp.max(st, axis=0, keepdims=True) + slope * float((d - cb) * tk)
                    if d == 0:
                        m_new = mx
                    else:
                        m_new = jnp.maximum(m_run[mp, cb], mx)
                        alpha[mp, cb, d] = jnp.exp2(m_run[mp, cb] - m_new)
                    m_run[mp, cb] = m_new
                    shift = m_new - slope * float((d - cb) * tk)
                    pb[p % 2][mp, :, ss] = jnp.exp2(st - shift).astype(BF16)
                if uc:
                    s, cb, d = uc[i]
                    cs = slice(cb * tk, (cb + 1) * tk)
                    pv = _dot(vt_ref[0, 0, :, d * tk:(d + 1) * tk],
                              pb[(p - 1) % 2][mp, :, s * tk:(s + 1) * tk], NN)
                    acc_ref[mp, :, cs] = pv if d == 0 else alpha[mp, cb, d] * acc_ref[mp, :, cs] + pv
            if uc and uc[i][2] == uc[i][1]:
                finish(uc[i][1])


def _attn(qt, kn, vt, proj, b_lambda, b_subln, batch, seq, tk=256):
    ncb = seq // tk
    assert seq % (2 * tk) == 0
    m = kn.shape[0]
    return pl.pallas_call(
        functools.partial(_attn_kernel, seq=seq, tk=tk),
        grid=(batch, HEADS),
        in_specs=[
            pl.BlockSpec((4, QK), lambda b, h: (0, 0)),
            pl.BlockSpec((1, HEAD), lambda b, h: (0, 0)),
            pl.BlockSpec((1, 1, HEAD, seq), lambda b, h: (b, h, 0, 0)),
            pl.BlockSpec((seq, HEAD), lambda b, h: (b, h)),
            pl.BlockSpec((1, 1, VT_ROWS, seq), lambda b, h: (b, h, 0, 0)),
            pl.BlockSpec((seq, HEAD), lambda b, h: (b, SEG_BG + h)),
        ],
        out_specs=pl.BlockSpec((seq, HEAD), lambda b, h: (b, h)),
        out_shape=jax.ShapeDtypeStruct((m, WIDTH), BF16),
        scratch_shapes=[
            pltpu.VMEM((2, tk, ncb // 2 * tk), F32), pltpu.VMEM((2, tk, ncb // 2 * tk), F32),
            pltpu.VMEM((2, tk, ncb // 2 * tk), BF16), pltpu.VMEM((2, tk, ncb // 2 * tk), BF16),
            pltpu.VMEM((2, VT_ROWS, seq), F32), pltpu.VMEM((2, 2 * HEAD, seq), BF16),
            pltpu.VMEM((tk, tk), F32),
        ],
        compiler_params=pltpu.CompilerParams(
            dimension_semantics=("arbitrary", "arbitrary"), vmem_limit_bytes=VMEM_LIMIT),
        name="attn",
    )(b_lambda, b_subln, qt, kn, vt, proj)


def _out_proj_kernel(ya_ref, yb_ref, ga_ref, gb_ref, x_ref, wa_ref, wb_ref, wo_ref, o_ref):
    ya = jnp.dot(ya_ref[...], wa_ref[...], preferred_element_type=F32)
    yb = jnp.dot(yb_ref[...], wb_ref[...], preferred_element_type=F32)
    mixed = jax.nn.sigmoid(ga_ref[...]) * ya + jax.nn.sigmoid(gb_ref[...]) * yb
    o_ref[...] = x_ref[...] + jnp.dot(mixed.astype(BF16), wo_ref[...],
                                      preferred_element_type=F32)


def _out_proj(ya, yb, proj, x2, wa, wb, wo, tm=256):
    m, d = x2.shape
    w = ya.shape[1]
    resident = functools.partial(pl.BlockSpec, pipeline_mode=pl.Buffered(1))
    return pl.pallas_call(
        _out_proj_kernel,
        grid=(m // tm,),
        in_specs=[
            pl.BlockSpec((tm, w), lambda i: (i, 0)),
            pl.BlockSpec((tm, w), lambda i: (i, 0)),
            pl.BlockSpec((tm, d), lambda i: (i, SEG_GATE_A)),
            pl.BlockSpec((tm, d), lambda i: (i, SEG_GATE_B)),
            pl.BlockSpec((tm, d), lambda i: (i, 0)),
            resident((w, d), lambda i: (0, 0)),
            resident((w, d), lambda i: (0, 0)),
            resident((d, d), lambda i: (0, 0)),
        ],
        out_specs=pl.BlockSpec((tm, d), lambda i: (i, 0)),
        out_shape=jax.ShapeDtypeStruct((m, d), F32),
        compiler_params=pltpu.CompilerParams(
            dimension_semantics=("arbitrary",), vmem_limit_bytes=VMEM_LIMIT),
        name="out_proj",
    )(ya, yb, proj, proj, x2, wa, wb, wo)


def kernel(x, norm_w, w_in, a_lower_bound, a_out_norm, b_q_norm, b_k_norm, b_lambda,
           b_subln, w_branch_a, w_branch_b, w_out):
    batch, seq, d = x.shape
    assert norm_w.shape[0] == 1 and a_lower_bound.shape == (2, WIDTH)
    assert d == 2 * WIDTH and seq % 512 == 0
    x2 = x.reshape(batch * seq, d)

    proj = _in_proj(x2, norm_w[0:1], w_in[0].astype(BF16))
    ya = _hgrn(proj, a_lower_bound, a_out_norm[0:1], batch, seq)
    qt, kn, vt = _qkv_prep(proj, jnp.tile(b_q_norm[0:1], (1, 2)),
                           jnp.tile(b_k_norm[0:1], (1, 2)), batch, seq)
    yb = _attn(qt, kn, vt, proj, b_lambda[0], b_subln[0:1], batch, seq)
    out = _out_proj(ya, yb, proj, x2, w_branch_a[0].astype(BF16),
                    w_branch_b[0].astype(BF16), w_out[0].astype(BF16))
    return out.reshape(batch, seq, d)
```

```python
import functools
import math

import jax
import jax.numpy as jnp
from jax import lax
from jax.experimental import pallas as pl
from jax.experimental.pallas import tpu as pltpu

F32 = jnp.float32
BF16 = jnp.bfloat16

EPS = 1e-6
CHUNK = 64
HEAD = 128
HEADS = 8
QK = 64
WIDTH = HEADS * HEAD
SUB = 8
LAMBDA_INIT = 0.8 - 0.6 * math.exp(-0.3 * 0)
NEG = -1e30
LOG2E = math.log2(math.e)
VT_ROWS = HEAD + 16

SEG_AQ, SEG_AF, SEG_AI, SEG_AG, SEG_BQ, SEG_BK, SEG_BV, SEG_BG = (s * HEADS for s in range(8))
SEG_GATE_A, SEG_GATE_B = 4, 5

VMEM_LIMIT = 56 * 1024 * 1024


def _dot(a, b, dims):
    return lax.dot_general(a, b, (dims, ((), ())), preferred_element_type=F32)


NN = ((1,), (0,))
NT = ((1,), (1,))
TN = ((0,), (0,))


def _in_proj_kernel(x_ref, nw_ref, w_ref, o_ref, h_ref):
    @pl.when(pl.program_id(1) == 0)
    def _():
        x = x_ref[...]
        ms = jnp.mean(x * x, axis=-1, keepdims=True)
        h_ref[...] = (x * lax.rsqrt(ms + EPS) * nw_ref[...]).astype(BF16)

    o_ref[...] = jnp.dot(h_ref[...], w_ref[...], preferred_element_type=F32)


def _in_proj(x2, norm_w, w_in_bf16, tm=1024, tn=2048):
    m, d = x2.shape
    n = w_in_bf16.shape[1]
    return pl.pallas_call(
        _in_proj_kernel,
        grid=(m // tm, n // tn),
        in_specs=[
            pl.BlockSpec((tm, d), lambda i, j: (i, 0)),
            pl.BlockSpec((1, d), lambda i, j: (0, 0)),
            pl.BlockSpec((d, tn), lambda i, j: (0, j)),
        ],
        out_specs=pl.BlockSpec((tm, tn), lambda i, j: (i, j)),
        out_shape=jax.ShapeDtypeStruct((m, n), F32),
        scratch_shapes=[pltpu.VMEM((tm, d), BF16)],
        compiler_params=pltpu.CompilerParams(
            dimension_semantics=("arbitrary", "arbitrary"), vmem_limit_bytes=VMEM_LIMIT),
        name="in_proj",
    )(x2, norm_w, w_in_bf16)


def _hgrn_kernel(alb_ref, gain_ref, aq_ref, af_ref, ai_ref, ag_ref, o_ref, st_ref, *, tt, hps):
    @pl.when(pl.program_id(2) == 0)
    def _():
        st_ref[...] = jnp.zeros_like(st_ref)

    for hh in range(hps):
        _hgrn_head(alb_ref, gain_ref, aq_ref, af_ref, ai_ref, ag_ref, o_ref, st_ref, hh, tt)


def _hgrn_head(alb_ref, gain_ref, aq_ref, af_ref, ai_ref, ag_ref, o_ref, st_ref, hh, tt):
    hs = slice(hh * HEAD, (hh + 1) * HEAD)
    a = alb_ref[:, hs]
    a0, a1 = a[0:1, :], a[1:2, :]
    amax = jnp.maximum(a0, a1)
    e0, e1 = jnp.exp(a0 - amax), jnp.exp(a1 - amax)
    lb = e0 / (e0 + e1)
    one_m_lb = 1.0 - lb
    gain = gain_ref[...]

    nch = tt // CHUNK
    nb = CHUNK // SUB
    row = lax.broadcasted_iota(jnp.int32, (CHUNK, CHUNK), 0)
    col = lax.broadcasted_iota(jnp.int32, (CHUNK, CHUNK), 1)
    tril = (row >= col).astype(BF16)

    def pair_mask(blk):
        return (((row & (blk - 1)) >= blk // 2) & ((col & (blk - 1)) < blk // 2)
                & ((row & -blk) == (col & -blk)))

    rowt = lax.broadcasted_iota(jnp.int32, (tt, HEAD), 0)
    tl4 = lax.broadcasted_iota(jnp.int32, (1, nb, SUB, CHUNK), 2)
    rel4 = (lax.broadcasted_iota(jnp.int32, (1, nb, SUB, CHUNK), 3)
            - SUB * lax.broadcasted_iota(jnp.int32, (1, nb, SUB, CHUNK), 1))

    def rows(x, c):
        return x[c * CHUNK:(c + 1) * CHUNK]

    def row_bcast(x, step, off):
        return jnp.concatenate(
            [jnp.broadcast_to(x[j * step + off:j * step + off + 1, :], (step, HEAD))
             for j in range(tt // step)], axis=0)

    z = af_ref[:, hs]
    qa = aq_ref[:, hs]
    v = ai_ref[:, hs].astype(BF16)
    g = ag_ref[:, hs]
    f = lb + one_m_lb * jax.nn.sigmoid(z)
    lf = jnp.log(f)
    k = 1.0 - f
    q = qa * jax.nn.sigmoid(qa)

    hi = lf.astype(BF16)
    lo = (lf - hi.astype(F32)).astype(BF16)
    b = jnp.concatenate(
        [_dot(tril, rows(hi, c), NN) + _dot(tril, rows(lo, c), NN) for c in range(nch)], axis=0)
    b_last = row_bcast(b, CHUNK, CHUNK - 1)
    qe = (q * jnp.exp(b)).astype(BF16)
    kd = (k * jnp.exp(b_last - b)).astype(BF16)
    dec = jnp.exp(b_last)

    def level(blk):
        half = blk // 2
        upper = (rowt & (blk - 1)) >= half
        d = b - row_bcast(b, blk, half)
        e = jnp.exp(jnp.where(upper, d, -d))
        xe = (jnp.where(upper, q, k) * e).astype(BF16)
        return [_dot(rows(xe, c), rows(xe, c), NT) for c in range(nch)]

    lv64, lv32, lv16 = level(64), level(32), level(16)
    m64, m32, m16 = pair_mask(64), pair_mask(32), pair_mask(16)

    shape4 = (nch, nb, SUB, HEAD)
    q4, f4 = q.reshape(shape4), f.reshape(shape4)
    gk = k.reshape(shape4)
    pt = jnp.zeros((nch, nb, SUB, CHUNK), F32)
    for j in range(SUB):
        if j:
            gk = f4 * pltpu.roll(gk, 1, axis=2)
        p = jnp.sum(q4 * gk, axis=3, keepdims=True)
        pt = jnp.where((rel4 == tl4 - j) & (tl4 >= j), p, pt)
    pt = pt.reshape(tt, CHUNK)

    intras = []
    for c in range(nch):
        sc = (jnp.where(m64, lv64[c], 0.0) + jnp.where(m32, lv32[c], 0.0)
              + jnp.where(m16, lv16[c], 0.0) + rows(pt, c))
        intras.append(_dot(sc.astype(BF16), rows(v, c), NN))
    updates = [_dot(rows(v, c), rows(kd, c), TN) for c in range(nch)]
    st = st_ref[hh]
    outs = []
    for c in range(nch):
        outs.append(intras[c] + _dot(rows(qe, c), st.astype(BF16), NT))
        st = st * dec[c * CHUNK:c * CHUNK + 1] + updates[c]
    st_ref[hh] = st

    out = jnp.concatenate(outs, axis=0)
    ms = jnp.mean(out * out, axis=-1, keepdims=True)
    y = out * lax.rsqrt(ms + EPS) * gain
    o_ref[:, hs] = (y * (g * jax.nn.sigmoid(g))).astype(o_ref.dtype)


def _hgrn(proj, a_lower_bound, a_out_norm, batch, seq, tt=512, hps=4):
    m = proj.shape[0]
    nt = seq // tt
    wid = hps * HEAD

    def seg(s):
        return pl.BlockSpec((tt, wid), lambda b, h, t, s=s: (b * nt + t, s // hps + h))

    return pl.pallas_call(
        functools.partial(_hgrn_kernel, tt=tt, hps=hps),
        grid=(batch, HEADS // hps, nt),
        in_specs=[
            pl.BlockSpec((2, wid), lambda b, h, t: (0, h)),
            pl.BlockSpec((1, HEAD), lambda b, h, t: (0, 0)),
            seg(SEG_AQ), seg(SEG_AF), seg(SEG_AI), seg(SEG_AG),
        ],
        out_specs=pl.BlockSpec((tt, wid), lambda b, h, t: (b * nt + t, h)),
        out_shape=jax.ShapeDtypeStruct((m, WIDTH), BF16),
        scratch_shapes=[pltpu.VMEM((hps, HEAD, HEAD), F32)],
        compiler_params=pltpu.CompilerParams(
            dimension_semantics=("arbitrary", "arbitrary", "arbitrary"),
            vmem_limit_bytes=VMEM_LIMIT),
        name="hgrn",
    )(a_lower_bound, a_out_norm, proj, proj, proj, proj)


def _qkv_prep_kernel(qg_ref, kg_ref, bq_ref, bk_ref, bv_ref, qt_ref, kn_ref, vt_ref):
    r = lax.broadcasted_iota(jnp.int32, (HEAD, HEAD), 0)
    c = lax.broadcasted_iota(jnp.int32, (HEAD, HEAD), 1)
    grp = ((r >> 6) == (c >> 6)).astype(BF16)

    def gnorm(x, gain):
        x2 = x * x
        hi = x2.astype(BF16)
        lo = (x2 - hi.astype(F32)).astype(BF16)
        ss = _dot(hi, grp, NN) + _dot(lo, grp, NN)
        return x * lax.rsqrt(ss * (1.0 / QK) + EPS) * gain

    tr = bq_ref.shape[0]
    pad_row = lax.broadcasted_iota(jnp.int32, (VT_ROWS - HEAD, tr), 0)
    ones_row = (pad_row == 0).astype(BF16)
    for h in range(HEADS):
        hs = slice(h * HEAD, (h + 1) * HEAD)
        qn = gnorm(bq_ref[:, hs], qg_ref[...]) * (QK ** -0.5 * LOG2E)
        qt_ref[0, h] = qn.T.astype(BF16)
        kn_ref[:, hs] = gnorm(bk_ref[:, hs], kg_ref[...]).astype(BF16)
        vt_ref[0, h, 0:HEAD, :] = bv_ref[:, hs].T.astype(BF16)
        vt_ref[0, h, HEAD:VT_ROWS, :] = ones_row


def _qkv_prep(proj, q_gain, k_gain, batch, seq, tr=512):
    m = proj.shape[0]
    nt = seq // tr

    def seg(s):
        return pl.BlockSpec((tr, WIDTH), lambda b, t, s=s: (b * nt + t, s // HEADS))

    def col_out(rows):
        return pl.BlockSpec((1, HEADS, rows, tr), lambda b, t: (b, 0, 0, t))

    return pl.pallas_call(
        _qkv_prep_kernel,
        grid=(batch, nt),
        in_specs=[
            pl.BlockSpec((1, HEAD), lambda b, t: (0, 0)),
            pl.BlockSpec((1, HEAD), lambda b, t: (0, 0)),
            seg(SEG_BQ), seg(SEG_BK), seg(SEG_BV),
        ],
        out_specs=[col_out(HEAD), pl.BlockSpec((tr, WIDTH), lambda b, t: (b * nt + t, 0)),
                   col_out(VT_ROWS)],
        out_shape=[jax.ShapeDtypeStruct((batch, HEADS, HEAD, seq), BF16),
                   jax.ShapeDtypeStruct((m, WIDTH), BF16),
                   jax.ShapeDtypeStruct((batch, HEADS, VT_ROWS, seq), BF16)],
        compiler_params=pltpu.CompilerParams(
            dimension_semantics=("arbitrary", "arbitrary"), vmem_limit_bytes=VMEM_LIMIT),
        name="qkv_prep",
    )(q_gain, k_gain, proj, proj, proj)


def _attn_kernel(lam_ref, sub_ref, qt_ref, k_ref, vt_ref, g_ref, o_ref,
                 sa0, sa1, pb0, pb1, acc_ref, qx_ref, dg_ref, *, seq, tk):
    sa = (sa0, sa1)
    pb = (pb0, pb1)
    ncb = seq // tk
    nslot = ncb // 2
    h = pl.program_id(1)
    slope = jnp.exp2(-(jnp.zeros((1, 1), F32) + (h + 1).astype(F32))) * LOG2E
    c_hi = slope.astype(BF16).astype(F32)
    c_lo = (slope - c_hi).astype(BF16).astype(F32)
    slope = c_hi + c_lo

    s_i = lax.broadcasted_iota(jnp.int32, (tk, tk), 0)
    t_i = lax.broadcasted_iota(jnp.int32, (tk, tk), 1)
    corr = slope * jnp.where(s_i <= t_i, 0, 2 * (t_i - s_i)).astype(F32)
    dg_ref[...] = jnp.where((s_i >> 6) <= (t_i >> 6), corr, NEG)

    rowd = lax.broadcasted_iota(jnp.int32, (HEAD, tk), 0)
    auxq = jnp.where(rowd == 0, c_hi, jnp.where(rowd == 1, c_lo, 0.0)).astype(BF16)
    for cb in range(ncb):
        cs = slice(cb * tk, (cb + 1) * tk)
        qs = qt_ref[0, 0, :, cs]
        zero = jnp.zeros_like(qs)
        qx_ref[0, 0:HEAD, cs] = jnp.where(rowd < QK, qs, zero)
        qx_ref[1, 0:HEAD, cs] = jnp.where(rowd >= QK, qs, zero)
        qx_ref[0, HEAD:2 * HEAD, cs] = auxq
        qx_ref[1, HEAD:2 * HEAD, cs] = auxq
    lane_k = lax.broadcasted_iota(jnp.int32, (tk, HEAD), 1)
    row_k = lax.broadcasted_iota(jnp.int32, (tk, HEAD), 0)
    auxk = jnp.where(lane_k <= 1, row_k, 0).astype(F32).astype(BF16)

    lv = lam_ref[...]
    lam = (jnp.exp(jnp.sum(lv[0:1, :] * lv[1:2, :], keepdims=True))
           - jnp.exp(jnp.sum(lv[2:3, :] * lv[3:4, :], keepdims=True)) + LAMBDA_INIT)

    def phase_units(p):
        if not 0 <= p <= ncb:
            return []
        return [(s, s, p) if p <= s else (s, ncb - 1 - s, p - s - 1) for s in range(nslot)]

    m_run, alpha = {}, {}

    def finish(cb):
        cs = slice(cb * tk, (cb + 1) * tk)
        ot = (acc_ref[0, 0:HEAD, cs] / acc_ref[0, HEAD:HEAD + 1, cs]
              - lam * (acc_ref[1, 0:HEAD, cs] / acc_ref[1, HEAD:HEAD + 1, cs]))
        o = ot.T
        ms = jnp.mean(o * o, axis=-1, keepdims=True)
        y = o * lax.rsqrt(ms + EPS) * sub_ref[...] * (1.0 - LAMBDA_INIT)
        g = g_ref[cs, :]
        o_ref[cs, :] = (y * (g * jax.nn.sigmoid(g))).astype(o_ref.dtype)

    for p in range(-1, ncb + 2):
        ua, ub, uc = phase_units(p + 1), phase_units(p), phase_units(p - 1)
        for i in range(nslot):
            for mp in range(2):
                if ua:
                    s, cb, d = ua[i]
                    kx = jnp.concatenate([k_ref[d * tk:(d + 1) * tk, :], auxk], axis=1)
                    sa[(p + 1) % 2][mp, :, s * tk:(s + 1) * tk] = _dot(
                        kx, qx_ref[mp, :, cb * tk:(cb + 1) * tk], NN)
                if ub:
                    s, cb, d = ub[i]
                    ss = slice(s * tk, (s + 1) * tk)
                    st = sa[p % 2][mp, :, ss]
                    if d == cb:
                        st = st + dg_ref[...]
                    mx = jnp.max(st, axis=0, keepdims=True) + slope * float((d - cb) * tk)
                    if d == 0:
                        m_new = mx
                    else:
                        m_new = jnp.maximum(m_run[mp, cb], mx)
                        alpha[mp, cb, d] = jnp.exp2(m_run[mp, cb] - m_new)
                    m_run[mp, cb] = m_new
                    shift = m_new - slope * float((d - cb) * tk)
                    pb[p % 2][mp, :, ss] = jnp.exp2(st - shift).astype(BF16)
                if uc:
                    s, cb, d = uc[i]
                    cs = slice(cb * tk, (cb + 1) * tk)
                    pv = _dot(vt_ref[0, 0, :, d * tk:(d + 1) * tk],
                              pb[(p - 1) % 2][mp, :, s * tk:(s + 1) * tk], NN)
                    acc_ref[mp, :, cs] = pv if d == 0 else alpha[mp, cb, d] * acc_ref[mp, :, cs] + pv
            if uc and uc[i][2] == uc[i][1]:
                finish(uc[i][1])


def _attn(qt, kn, vt, proj, b_lambda, b_subln, batch, seq, tk=256):
    ncb = seq // tk
    assert seq % (2 * tk) == 0
    m = kn.shape[0]
    return pl.pallas_call(
        functools.partial(_attn_kernel, seq=seq, tk=tk),
        grid=(batch, HEADS),
        in_specs=[
            pl.BlockSpec((4, QK), lambda b, h: (0, 0)),
            pl.BlockSpec((1, HEAD), lambda b, h: (0, 0)),
            pl.BlockSpec((1, 1, HEAD, seq), lambda b, h: (b, h, 0, 0)),
            pl.BlockSpec((seq, HEAD), lambda b, h: (b, h)),
            pl.BlockSpec((1, 1, VT_ROWS, seq), lambda b, h: (b, h, 0, 0)),
            pl.BlockSpec((seq, HEAD), lambda b, h: (b, SEG_BG + h)),
        ],
        out_specs=pl.BlockSpec((seq, HEAD), lambda b, h: (b, h)),
        out_shape=jax.ShapeDtypeStruct((m, WIDTH), BF16),
        scratch_shapes=[
            pltpu.VMEM((2, tk, ncb // 2 * tk), F32), pltpu.VMEM((2, tk, ncb // 2 * tk), F32),
            pltpu.VMEM((2, tk, ncb // 2 * tk), BF16), pltpu.VMEM((2, tk, ncb // 2 * tk), BF16),
            pltpu.VMEM((2, VT_ROWS, seq), F32), pltpu.VMEM((2, 2 * HEAD, seq), BF16),
            pltpu.VMEM((tk, tk), F32),
        ],
        compiler_params=pltpu.CompilerParams(
            dimension_semantics=("arbitrary", "arbitrary"), vmem_limit_bytes=VMEM_LIMIT),
        name="attn",
    )(b_lambda, b_subln, qt, kn, vt, proj)


def _out_proj_kernel(ya_ref, yb_ref, ga_ref, gb_ref, x_ref, wa_ref, wb_ref, wo_ref, o_ref):
    ya = jnp.dot(ya_ref[...], wa_ref[...], preferred_element_type=F32)
    yb = jnp.dot(yb_ref[...], wb_ref[...], preferred_element_type=F32)
    mixed = jax.nn.sigmoid(ga_ref[...]) * ya + jax.nn.sigmoid(gb_ref[...]) * yb
    o_ref[...] = x_ref[...] + jnp.dot(mixed.astype(BF16), wo_ref[...],
                                      preferred_element_type=F32)


def _out_proj(ya, yb, proj, x2, wa, wb, wo, tm=256):
    m, d = x2.shape
    w = ya.shape[1]
    resident = functools.partial(pl.BlockSpec, pipeline_mode=pl.Buffered(1))
    return pl.pallas_call(
        _out_proj_kernel,
        grid=(m // tm,),
        in_specs=[
            pl.BlockSpec((tm, w), lambda i: (i, 0)),
            pl.BlockSpec((tm, w), lambda i: (i, 0)),
            pl.BlockSpec((tm, d), lambda i: (i, SEG_GATE_A)),
            pl.BlockSpec((tm, d), lambda i: (i, SEG_GATE_B)),
            pl.BlockSpec((tm, d), lambda i: (i, 0)),
            resident((w, d), lambda i: (0, 0)),
            resident((w, d), lambda i: (0, 0)),
            resident((d, d), lambda i: (0, 0)),
        ],
        out_specs=pl.BlockSpec((tm, d), lambda i: (i, 0)),
        out_shape=jax.ShapeDtypeStruct((m, d), F32),
        compiler_params=pltpu.CompilerParams(
            dimension_semantics=("arbitrary",), vmem_limit_bytes=VMEM_LIMIT),
        name="out_proj",
    )(ya, yb, proj, proj, x2, wa, wb, wo)


def kernel(x, norm_w, w_in, a_lower_bound, a_out_norm, b_q_norm, b_k_norm, b_lambda,
           b_subln, w_branch_a, w_branch_b, w_out):
    batch, seq, d = x.shape
    assert norm_w.shape[0] == 1 and a_lower_bound.shape == (2, WIDTH)
    assert d == 2 * WIDTH and seq % 512 == 0
    x2 = x.reshape(batch * seq, d)

    proj = _in_proj(x2, norm_w[0:1], w_in[0].astype(BF16))
    ya = _hgrn(proj, a_lower_bound, a_out_norm[0:1], batch, seq)
    qt, kn, vt = _qkv_prep(proj, jnp.tile(b_q_norm[0:1], (1, 2)),
                           jnp.tile(b_k_norm[0:1], (1, 2)), batch, seq)
    yb = _attn(qt, kn, vt, proj, b_lambda[0], b_subln[0:1], batch, seq)
    out = _out_proj(ya, yb, proj, x2, w_branch_a[0].astype(BF16),
                    w_branch_b[0].astype(BF16), w_out[0].astype(BF16))
    return out.reshape(batch, seq, d)
```

```python
import functools
import math

import jax
import jax.numpy as jnp
from jax import lax
from jax.experimental import pallas as pl
from jax.experimental.pallas import tpu as pltpu

F32 = jnp.float32
BF16 = jnp.bfloat16

EPS = 1e-6
CHUNK = 64
HEAD = 128
HEADS = 8
QK = 64
WIDTH = HEADS * HEAD
SUB = 8
LAMBDA_INIT = 0.8 - 0.6 * math.exp(-0.3 * 0)
NEG = -1e30
LOG2E = math.log2(math.e)
VT_ROWS = HEAD + 16
HGRN_PIECES = 12

SEG_AQ, SEG_AF, SEG_AI, SEG_AG, SEG_BQ, SEG_BK, SEG_BV, SEG_BG = (s * HEADS for s in range(8))
SEG_GATE_A, SEG_GATE_B = 4, 5

VMEM_LIMIT = 56 * 1024 * 1024


def _dot(a, b, dims):
    return lax.dot_general(a, b, (dims, ((), ())), preferred_element_type=F32)


NN = ((1,), (0,))
NT = ((1,), (1,))
TN = ((0,), (0,))


def _in_proj_kernel(x_ref, nw_ref, w_ref, o_ref, h_ref):
    @pl.when(pl.program_id(1) == 0)
    def _():
        x = x_ref[...]
        ms = jnp.mean(x * x, axis=-1, keepdims=True)
        h_ref[...] = (x * lax.rsqrt(ms + EPS) * nw_ref[...]).astype(BF16)

    o_ref[...] = jnp.dot(h_ref[...], w_ref[...], preferred_element_type=F32).astype(o_ref.dtype)


def _in_proj(x2, norm_w, w_in_bf16, tm=1024, tn=2048):
    m, d = x2.shape
    n = w_in_bf16.shape[1]
    return pl.pallas_call(
        _in_proj_kernel,
        grid=(m // tm, n // tn),
        in_specs=[
            pl.BlockSpec((tm, d), lambda i, j: (i, 0)),
            pl.BlockSpec((1, d), lambda i, j: (0, 0)),
            pl.BlockSpec((d, tn), lambda i, j: (0, j)),
        ],
        out_specs=pl.BlockSpec((tm, tn), lambda i, j: (i, j)),
        out_shape=jax.ShapeDtypeStruct((m, n), BF16),
        scratch_shapes=[pltpu.VMEM((tm, d), BF16)],
        compiler_params=pltpu.CompilerParams(
            dimension_semantics=("arbitrary", "arbitrary"), vmem_limit_bytes=VMEM_LIMIT),
        name="in_proj",
    )(x2, norm_w, w_in_bf16)


def _hgrn_pieces(alb_ref, gain_ref, aq_ref, af_ref, ai_ref, ag_ref, o_ref, seq, tt):
    a = alb_ref[...]
    a0, a1 = a[0:1, :], a[1:2, :]
    amax = jnp.maximum(a0, a1)
    e0, e1 = jnp.exp(a0 - amax), jnp.exp(a1 - amax)
    lb = e0 / (e0 + e1)
    one_m_lb = 1.0 - lb
    gain = gain_ref[...]

    nch = tt // CHUNK
    nb = CHUNK // SUB
    row = lax.broadcasted_iota(jnp.int32, (CHUNK, CHUNK), 0)
    col = lax.broadcasted_iota(jnp.int32, (CHUNK, CHUNK), 1)
    tril = (row >= col).astype(BF16)

    def pair_mask(blk):
        return (((row & (blk - 1)) >= blk // 2) & ((col & (blk - 1)) < blk // 2)
                & ((row & -blk) == (col & -blk)))

    rowt = lax.broadcasted_iota(jnp.int32, (tt, HEAD), 0)
    tl4 = lax.broadcasted_iota(jnp.int32, (1, nb, SUB, CHUNK), 2)
    rel4 = (lax.broadcasted_iota(jnp.int32, (1, nb, SUB, CHUNK), 3)
            - SUB * lax.broadcasted_iota(jnp.int32, (1, nb, SUB, CHUNK), 1))

    def rows(x, c):
        return x[c * CHUNK:(c + 1) * CHUNK]

    def row_bcast(x, step, off):
        return jnp.concatenate(
            [jnp.broadcast_to(x[j * step + off:j * step + off + 1, :], (step, HEAD))
             for j in range(tt // step)], axis=0)

    st = jnp.zeros((HEAD, HEAD), F32)
    for t in range(seq // tt):
        rs = slice(t * tt, (t + 1) * tt)
        z = af_ref[rs, :].astype(F32)
        qa = aq_ref[rs, :].astype(F32)
        v = ai_ref[rs, :]
        g = ag_ref[rs, :].astype(F32)
        f = lb + one_m_lb * jax.nn.sigmoid(z)
        lf = jnp.log(f)
        k = 1.0 - f
        q = qa * jax.nn.sigmoid(qa)
        yield

        hi = lf.astype(BF16)
        lo = (lf - hi.astype(F32)).astype(BF16)
        b = jnp.concatenate(
            [_dot(tril, rows(hi, c), NN) + _dot(tril, rows(lo, c), NN) for c in range(nch)],
            axis=0)
        b_last = row_bcast(b, CHUNK, CHUNK - 1)
        qe = (q * jnp.exp(b)).astype(BF16)
        kd = (k * jnp.exp(b_last - b)).astype(BF16)
        dec = jnp.exp(b_last)
        yield

        def level(blk):
            half = blk // 2
            upper = (rowt & (blk - 1)) >= half
            d = b - row_bcast(b, blk, half)
            e = jnp.exp(jnp.where(upper, d, -d))
            xe = (jnp.where(upper, q, k) * e).astype(BF16)
            return [_dot(rows(xe, c), rows(xe, c), NT) for c in range(nch)]

        lv64 = level(64)
        yield
        lv32 = level(32)
        yield
        lv16 = level(16)
        yield
        m64, m32, m16 = pair_mask(64), pair_mask(32), pair_mask(16)

        shape4 = (nch, nb, SUB, HEAD)
        q4, f4 = q.reshape(shape4), f.reshape(shape4)
        gk = k.reshape(shape4)
        pt = jnp.zeros((nch, nb, SUB, CHUNK), F32)
        for j in range(SUB):
            if j:
                gk = f4 * pltpu.roll(gk, 1, axis=2)
            p = jnp.sum(q4 * gk, axis=3, keepdims=True)
            pt = jnp.where((rel4 == tl4 - j) & (tl4 >= j), p, pt)
            if j % 2:
                yield
        pt = pt.reshape(tt, CHUNK)

        intras = []
        for c in range(nch):
            sc = (jnp.where(m64, lv64[c], 0.0) + jnp.where(m32, lv32[c], 0.0)
                  + jnp.where(m16, lv16[c], 0.0) + rows(pt, c))
            intras.append(_dot(sc.astype(BF16), rows(v, c), NN))
        updates = [_dot(rows(v, c), rows(kd, c), TN) for c in range(nch)]
        yield
        outs = []
        for c in range(nch):
            outs.append(intras[c] + _dot(rows(qe, c), st.astype(BF16), NT))
            st = st * dec[c * CHUNK:c * CHUNK + 1] + updates[c]
        yield

        out = jnp.concatenate(outs, axis=0)
        ms = jnp.mean(out * out, axis=-1, keepdims=True)
        y = out * lax.rsqrt(ms + EPS) * gain
        o_ref[rs, :] = (y * (g * jax.nn.sigmoid(g))).astype(o_ref.dtype)
        yield


def _qkv_prep_kernel(qg_ref, kg_ref, bq_ref, bk_ref, bv_ref, qt_ref, kn_ref, vt_ref):
    r = lax.broadcasted_iota(jnp.int32, (HEAD, HEAD), 0)
    c = lax.broadcasted_iota(jnp.int32, (HEAD, HEAD), 1)
    grp = ((r >> 6) == (c >> 6)).astype(BF16)

    def gnorm(x, gain):
        x2 = x * x
        hi = x2.astype(BF16)
        lo = (x2 - hi.astype(F32)).astype(BF16)
        ss = _dot(hi, grp, NN) + _dot(lo, grp, NN)
        return x * lax.rsqrt(ss * (1.0 / QK) + EPS) * gain

    tr = bq_ref.shape[0]
    pad_row = lax.broadcasted_iota(jnp.int32, (VT_ROWS - HEAD, tr), 0)
    ones_row = (pad_row == 0).astype(BF16)
    for h in range(HEADS):
        hs = slice(h * HEAD, (h + 1) * HEAD)
        qn = gnorm(bq_ref[:, hs].astype(F32), qg_ref[...]) * (QK ** -0.5 * LOG2E)
        qt_ref[0, h] = qn.T.astype(BF16)
        kn_ref[:, hs] = gnorm(bk_ref[:, hs].astype(F32), kg_ref[...]).astype(BF16)
        vt_ref[0, h, 0:HEAD, :] = bv_ref[:, hs].astype(F32).T.astype(BF16)
        vt_ref[0, h, HEAD:VT_ROWS, :] = ones_row


def _qkv_prep(proj, q_gain, k_gain, batch, seq, tr=512):
    m = proj.shape[0]
    nt = seq // tr

    def seg(s):
        return pl.BlockSpec((tr, WIDTH), lambda b, t, s=s: (b * nt + t, s // HEADS))

    def col_out(rows):
        return pl.BlockSpec((1, HEADS, rows, tr), lambda b, t: (b, 0, 0, t))

    return pl.pallas_call(
        _qkv_prep_kernel,
        grid=(batch, nt),
        in_specs=[
            pl.BlockSpec((1, HEAD), lambda b, t: (0, 0)),
            pl.BlockSpec((1, HEAD), lambda b, t: (0, 0)),
            seg(SEG_BQ), seg(SEG_BK), seg(SEG_BV),
        ],
        out_specs=[col_out(HEAD), pl.BlockSpec((tr, WIDTH), lambda b, t: (b * nt + t, 0)),
                   col_out(VT_ROWS)],
        out_shape=[jax.ShapeDtypeStruct((batch, HEADS, HEAD, seq), BF16),
                   jax.ShapeDtypeStruct((m, WIDTH), BF16),
                   jax.ShapeDtypeStruct((batch, HEADS, VT_ROWS, seq), BF16)],
        compiler_params=pltpu.CompilerParams(
            dimension_semantics=("arbitrary", "arbitrary"), vmem_limit_bytes=VMEM_LIMIT),
        name="qkv_prep",
    )(q_gain, k_gain, proj, proj, proj)


def _mix_kernel(lam_ref, sub_ref, qt_ref, k_ref, vt_ref, g_ref,
                alb_ref, gain_ref, aq_ref, af_ref, ai_ref, ag_ref, o_ref, oa_ref,
                sa0, sa1, pb0, pb1, acc_ref, dg_ref, *, seq, tk, tt):
    hgrn = _hgrn_pieces(alb_ref, gain_ref, aq_ref, af_ref, ai_ref, ag_ref, oa_ref, seq, tt)
    n_pieces = HGRN_PIECES * (seq // tt)
    sa = (sa0, sa1)
    pb = (pb0, pb1)
    ncb = seq // tk
    nslot = ncb // 2
    h = pl.program_id(1)
    slope = jnp.exp2(-(jnp.zeros((1, 1), F32) + (h + 1).astype(F32))) * LOG2E
    c_hi = slope.astype(BF16).astype(F32)
    c_lo = (slope - c_hi).astype(BF16).astype(F32)
    slope = c_hi + c_lo

    s_i = lax.broadcasted_iota(jnp.int32, (tk, tk), 0)
    t_i = lax.broadcasted_iota(jnp.int32, (tk, tk), 1)
    corr = slope * jnp.where(s_i <= t_i, 0, 2 * (t_i - s_i)).astype(F32)
    dg_ref[...] = jnp.where((s_i >> 6) <= (t_i >> 6), corr, NEG)

    rowd = lax.broadcasted_iota(jnp.int32, (HEAD, tk), 0)
    auxq = jnp.where(rowd == 0, c_hi, jnp.where(rowd == 1, c_lo, 0.0)).astype(BF16)
    qzero = jnp.zeros((HEAD, tk), BF16)

    def qx(mp, cb):
        qs = qt_ref[0, 0, :, cb * tk:(cb + 1) * tk]
        keep = rowd < QK if mp == 0 else rowd >= QK
        return jnp.concatenate([jnp.where(keep, qs, qzero), auxq], axis=0)

    lane_k = lax.broadcasted_iota(jnp.int32, (tk, HEAD), 1)
    row_k = lax.broadcasted_iota(jnp.int32, (tk, HEAD), 0)
    auxk = jnp.where(lane_k <= 1, row_k, 0).astype(F32).astype(BF16)

    lv = lam_ref[...]
    lam = (jnp.exp(jnp.sum(lv[0:1, :] * lv[1:2, :], keepdims=True))
           - jnp.exp(jnp.sum(lv[2:3, :] * lv[3:4, :], keepdims=True)) + LAMBDA_INIT)

    def phase_units(p):
        if not 0 <= p <= ncb:
            return []
        return [(s, s, p) if p <= s else (s, ncb - 1 - s, p - s - 1) for s in range(nslot)]

    m_run, alpha = {}, {}

    def finish(cb):
        cs = slice(cb * tk, (cb + 1) * tk)
        ot = (acc_ref[0, 0:HEAD, cs] / acc_ref[0, HEAD:HEAD + 1, cs]
              - lam * (acc_ref[1, 0:HEAD, cs] / acc_ref[1, HEAD:HEAD + 1, cs]))
        o = ot.T
        ms = jnp.mean(o * o, axis=-1, keepdims=True)
        y = o * lax.rsqrt(ms + EPS) * sub_ref[...] * (1.0 - LAMBDA_INIT)
        g = g_ref[cs, :].astype(F32)
        o_ref[cs, :] = (y * (g * jax.nn.sigmoid(g))).astype(o_ref.dtype)

    for p in range(-1, ncb + 2):
        ua, ub, uc = phase_units(p + 1), phase_units(p), phase_units(p - 1)
        for i in range(nslot):
            for mp in range(2):
                if ua:
                    s, cb, d = ua[i]
                    kx = jnp.concatenate([k_ref[d * tk:(d + 1) * tk, :], auxk], axis=1)
                    sa[(p + 1) % 2][mp, :, s * tk:(s + 1) * tk] = _dot(kx, qx(mp, cb), NN)
                if ub:
                    s, cb, d = ub[i]
                    ss = slice(s * tk, (s + 1) * tk)
                    st = sa[p % 2][mp, :, ss]
                    if d == cb:
                        st = st + dg_ref[...]
                    mx = jnp.max(st, axis=0, keepdims=True) + slope * float((d - cb) * tk)
                    if d == 0:
                        m_new = mx
                    else:
                        m_new = jnp.maximum(m_run[mp, cb], mx)
                        alpha[mp, cb, d] = jnp.exp2(m_run[mp, cb] - m_new)
                    m_run[mp, cb] = m_new
                    shift = m_new - slope * float((d - cb) * tk)
                    pb[p % 2][mp, :, ss] = jnp.exp2(st - shift).astype(BF16)
                if uc:
                    s, cb, d = uc[i]
                    cs = slice(cb * tk, (cb + 1) * tk)
                    pv = _dot(vt_ref[0, 0, :, d * tk:(d + 1) * tk],
                              pb[(p - 1) % 2][mp, :, s * tk:(s + 1) * tk], NN)
                    acc_ref[mp, :, cs] = pv if d == 0 else alpha[mp, cb, d] * acc_ref[mp, :, cs] + pv
            if uc and uc[i][2] == uc[i][1]:
                finish(uc[i][1])
            n_iter = (p + 1) * nslot + i + 1
            for _ in range(n_iter * n_pieces // ((ncb + 3) * nslot)
                           - (n_iter - 1) * n_pieces // ((ncb + 3) * nslot)):
                next(hgrn)


def _mix(qt, kn, vt, proj, b_lambda, b_subln, a_lower_bound, a_out_norm, batch, seq,
         tk=256, tt=512):
    ncb = seq // tk
    assert seq % (2 * tk) == 0 and seq % tt == 0
    m = kn.shape[0]

    def seg(s):
        return pl.BlockSpec((seq, HEAD), lambda b, h, s=s: (b, s + h))

    head_out = pl.BlockSpec((seq, HEAD), lambda b, h: (b, h))
    return pl.pallas_call(
        functools.partial(_mix_kernel, seq=seq, tk=tk, tt=tt),
        grid=(batch, HEADS),
        in_specs=[
            pl.BlockSpec((4, QK), lambda b, h: (0, 0)),
            pl.BlockSpec((1, HEAD), lambda b, h: (0, 0)),
            pl.BlockSpec((1, 1, HEAD, seq), lambda b, h: (b, h, 0, 0)),
            pl.BlockSpec((seq, HEAD), lambda b, h: (b, h)),
            pl.BlockSpec((1, 1, VT_ROWS, seq), lambda b, h: (b, h, 0, 0)),
            seg(SEG_BG),
            pl.BlockSpec((2, HEAD), lambda b, h: (0, h)),
            pl.BlockSpec((1, HEAD), lambda b, h: (0, 0)),
            seg(SEG_AQ), seg(SEG_AF), seg(SEG_AI), seg(SEG_AG),
        ],
        out_specs=[head_out, head_out],
        out_shape=[jax.ShapeDtypeStruct((m, WIDTH), BF16), jax.ShapeDtypeStruct((m, WIDTH), BF16)],
        scratch_shapes=[
            pltpu.VMEM((2, tk, ncb // 2 * tk), F32), pltpu.VMEM((2, tk, ncb // 2 * tk), F32),
            pltpu.VMEM((2, tk, ncb // 2 * tk), BF16), pltpu.VMEM((2, tk, ncb // 2 * tk), BF16),
            pltpu.VMEM((2, VT_ROWS, seq), F32), pltpu.VMEM((tk, tk), F32),
        ],
        compiler_params=pltpu.CompilerParams(
            dimension_semantics=("arbitrary", "arbitrary"), vmem_limit_bytes=VMEM_LIMIT),
        name="mix",
    )(b_lambda, b_subln, qt, kn, vt, proj, a_lower_bound, a_out_norm, proj, proj, proj, proj)


def _out_proj_kernel(ya_ref, yb_ref, ga_ref, gb_ref, x_ref, wa_ref, wb_ref, wo_ref, o_ref):
    ya = jnp.dot(ya_ref[...], wa_ref[...], preferred_element_type=F32)
    yb = jnp.dot(yb_ref[...], wb_ref[...], preferred_element_type=F32)
    mixed = (jax.nn.sigmoid(ga_ref[...].astype(F32)) * ya
             + jax.nn.sigmoid(gb_ref[...].astype(F32)) * yb)
    o_ref[...] = x_ref[...] + jnp.dot(mixed.astype(BF16), wo_ref[...],
                                      preferred_element_type=F32)


def _out_proj(ya, yb, proj, x2, wa, wb, wo, tm=512):
    m, d = x2.shape
    w = ya.shape[1]
    resident = functools.partial(pl.BlockSpec, pipeline_mode=pl.Buffered(1))
    return pl.pallas_call(
        _out_proj_kernel,
        grid=(m // tm,),
        in_specs=[
            pl.BlockSpec((tm, w), lambda i: (i, 0)),
            pl.BlockSpec((tm, w), lambda i: (i, 0)),
            pl.BlockSpec((tm, d), lambda i: (i, SEG_GATE_A)),
            pl.BlockSpec((tm, d), lambda i: (i, SEG_GATE_B)),
            pl.BlockSpec((tm, d), lambda i: (i, 0)),
            resident((w, d), lambda i: (0, 0)),
            resident((w, d), lambda i: (0, 0)),
            resident((d, d), lambda i: (0, 0)),
        ],
        out_specs=pl.BlockSpec((tm, d), lambda i: (i, 0)),
        out_shape=jax.ShapeDtypeStruct((m, d), F32),
        compiler_params=pltpu.CompilerParams(
            dimension_semantics=("arbitrary",), vmem_limit_bytes=VMEM_LIMIT),
        name="out_proj",
    )(ya, yb, proj, proj, x2, wa, wb, wo)


def kernel(x, norm_w, w_in, a_lower_bound, a_out_norm, b_q_norm, b_k_norm, b_lambda,
           b_subln, w_branch_a, w_branch_b, w_out):
    batch, seq, d = x.shape
    assert norm_w.shape[0] == 1 and a_lower_bound.shape == (2, WIDTH)
    assert d == 2 * WIDTH and seq % 512 == 0
    x2 = x.reshape(batch * seq, d)

    proj = _in_proj(x2, norm_w[0:1], w_in[0].astype(BF16))
    qt, kn, vt = _qkv_prep(proj, jnp.tile(b_q_norm[0:1], (1, 2)),
                           jnp.tile(b_k_norm[0:1], (1, 2)), batch, seq)
    yb, ya = _mix(qt, kn, vt, proj, b_lambda[0], b_subln[0:1], a_lower_bound, a_out_norm[0:1],
                  batch, seq)
    out = _out_proj(ya, yb, proj, x2, w_branch_a[0].astype(BF16),
                    w_branch_b[0].astype(BF16), w_out[0].astype(BF16))
    return out.reshape(batch, seq, d)
```

```python
import functools
import math

import jax
import jax.numpy as jnp
from jax import lax
from jax.experimental import pallas as pl
from jax.experimental.pallas import tpu as pltpu

F32 = jnp.float32
BF16 = jnp.bfloat16

EPS = 1e-6
CHUNK = 64
HEAD = 128
HEADS = 8
QK = 64
WIDTH = HEADS * HEAD
SUB = 8
LAMBDA_INIT = 0.8 - 0.6 * math.exp(-0.3 * 0)
NEG = -1e30
LOG2E = math.log2(math.e)
VT_ROWS = HEAD + 16

SEG_AQ, SEG_AF, SEG_AI, SEG_AG, SEG_BQ, SEG_BK, SEG_BV, SEG_BG = (s * HEADS for s in range(8))
SEG_GATE_A, SEG_GATE_B = 4, 5

VMEM_LIMIT = 56 * 1024 * 1024


def _dot(a, b, dims):
    return lax.dot_general(a, b, (dims, ((), ())), preferred_element_type=F32)


NN = ((1,), (0,))
NT = ((1,), (1,))
TN = ((0,), (0,))


def _in_proj_kernel(x_ref, nw_ref, w_ref, o_ref, h_ref):
    @pl.when(pl.program_id(1) == 0)
    def _():
        x = x_ref[...]
        ms = jnp.mean(x * x, axis=-1, keepdims=True)
        h_ref[...] = (x * lax.rsqrt(ms + EPS) * nw_ref[...]).astype(BF16)

    o_ref[...] = jnp.dot(h_ref[...], w_ref[...], preferred_element_type=F32).astype(o_ref.dtype)


def _in_proj(x2, norm_w, w_in_bf16, tm=1024, tn=2048):
    m, d = x2.shape
    n = w_in_bf16.shape[1]
    return pl.pallas_call(
        _in_proj_kernel,
        grid=(m // tm, n // tn),
        in_specs=[
            pl.BlockSpec((tm, d), lambda i, j: (i, 0)),
            pl.BlockSpec((1, d), lambda i, j: (0, 0)),
            pl.BlockSpec((d, tn), lambda i, j: (0, j)),
        ],
        out_specs=pl.BlockSpec((tm, tn), lambda i, j: (i, j)),
        out_shape=jax.ShapeDtypeStruct((m, n), BF16),
        scratch_shapes=[pltpu.VMEM((tm, d), BF16)],
        compiler_params=pltpu.CompilerParams(
            dimension_semantics=("arbitrary", "arbitrary"), vmem_limit_bytes=VMEM_LIMIT),
        name="in_proj",
    )(x2, norm_w, w_in_bf16)


def _hgrn_kernel(alb_ref, gain_ref, aq_ref, af_ref, ai_ref, ag_ref, o_ref, st_ref, *, tt, hps):
    @pl.when(pl.program_id(2) == 0)
    def _():
        st_ref[...] = jnp.zeros_like(st_ref)

    for hh in range(hps):
        _hgrn_head(alb_ref, gain_ref, aq_ref, af_ref, ai_ref, ag_ref, o_ref, st_ref, hh, tt)


def _hgrn_head(alb_ref, gain_ref, aq_ref, af_ref, ai_ref, ag_ref, o_ref, st_ref, hh, tt):
    hs = slice(hh * HEAD, (hh + 1) * HEAD)
    a = alb_ref[:, hs]
    a0, a1 = a[0:1, :], a[1:2, :]
    amax = jnp.maximum(a0, a1)
    e0, e1 = jnp.exp(a0 - amax), jnp.exp(a1 - amax)
    lb = e0 / (e0 + e1)
    one_m_lb = 1.0 - lb
    gain = gain_ref[...]

    nch = tt // CHUNK
    nb = CHUNK // SUB
    row = lax.broadcasted_iota(jnp.int32, (CHUNK, CHUNK), 0)
    col = lax.broadcasted_iota(jnp.int32, (CHUNK, CHUNK), 1)
    tril = (row >= col).astype(BF16)

    def pair_mask(blk):
        return (((row & (blk - 1)) >= blk // 2) & ((col & (blk - 1)) < blk // 2)
                & ((row & -blk) == (col & -blk)))

    rowt = lax.broadcasted_iota(jnp.int32, (tt, HEAD), 0)
    tl4 = lax.broadcasted_iota(jnp.int32, (1, nb, SUB, CHUNK), 2)
    rel4 = (lax.broadcasted_iota(jnp.int32, (1, nb, SUB, CHUNK), 3)
            - SUB * lax.broadcasted_iota(jnp.int32, (1, nb, SUB, CHUNK), 1))

    def rows(x, c):
        return x[c * CHUNK:(c + 1) * CHUNK]

    def row_bcast(x, step, off):
        return jnp.concatenate(
            [jnp.broadcast_to(x[j * step + off:j * step + off + 1, :], (step, HEAD))
             for j in range(tt // step)], axis=0)

    z = af_ref[:, hs].astype(F32)
    qa = aq_ref[:, hs].astype(F32)
    v = ai_ref[:, hs]
    g = ag_ref[:, hs].astype(F32)
    f = lb + one_m_lb * jax.nn.sigmoid(z)
    lf = jnp.log(f)
    k = 1.0 - f
    q = qa * jax.nn.sigmoid(qa)

    hi = lf.astype(BF16)
    lo = (lf - hi.astype(F32)).astype(BF16)
    b = jnp.concatenate(
        [_dot(tril, rows(hi, c), NN) + _dot(tril, rows(lo, c), NN) for c in range(nch)], axis=0)
    b_last = row_bcast(b, CHUNK, CHUNK - 1)
    qe = (q * jnp.exp(b)).astype(BF16)
    kd = (k * jnp.exp(b_last - b)).astype(BF16)
    dec = jnp.exp(b_last)

    def level(blk):
        half = blk // 2
        upper = (rowt & (blk - 1)) >= half
        d = b - row_bcast(b, blk, half)
        e = jnp.exp(jnp.where(upper, d, -d))
        xe = (jnp.where(upper, q, k) * e).astype(BF16)
        return [_dot(rows(xe, c), rows(xe, c), NT) for c in range(nch)]

    lv64, lv32, lv16 = level(64), level(32), level(16)
    m64, m32, m16 = pair_mask(64), pair_mask(32), pair_mask(16)

    shape4 = (nch, nb, SUB, HEAD)
    q4, f4 = q.reshape(shape4), f.reshape(shape4)
    gk = k.reshape(shape4)
    pt = jnp.zeros((nch, nb, SUB, CHUNK), F32)
    for j in range(SUB):
        if j:
            gk = f4 * pltpu.roll(gk, 1, axis=2)
        p = jnp.sum(q4 * gk, axis=3, keepdims=True)
        pt = jnp.where((rel4 == tl4 - j) & (tl4 >= j), p, pt)
    pt = pt.reshape(tt, CHUNK)

    intras = []
    for c in range(nch):
        sc = (jnp.where(m64, lv64[c], 0.0) + jnp.where(m32, lv32[c], 0.0)
              + jnp.where(m16, lv16[c], 0.0) + rows(pt, c))
        intras.append(_dot(sc.astype(BF16), rows(v, c), NN))
    updates = [_dot(rows(v, c), rows(kd, c), TN) for c in range(nch)]
    st = st_ref[hh]
    outs = []
    for c in range(nch):
        outs.append(intras[c] + _dot(rows(qe, c), st.astype(BF16), NT))
        st = st * dec[c * CHUNK:c * CHUNK + 1] + updates[c]
    st_ref[hh] = st

    out = jnp.concatenate(outs, axis=0)
    ms = jnp.mean(out * out, axis=-1, keepdims=True)
    y = out * lax.rsqrt(ms + EPS) * gain
    o_ref[:, hs] = (y * (g * jax.nn.sigmoid(g))).astype(o_ref.dtype)


def _hgrn(proj, a_lower_bound, a_out_norm, batch, seq, tt=512, hps=4):
    m = proj.shape[0]
    nt = seq // tt
    wid = hps * HEAD

    def seg(s):
        return pl.BlockSpec((tt, wid), lambda b, h, t, s=s: (b * nt + t, s // hps + h))

    return pl.pallas_call(
        functools.partial(_hgrn_kernel, tt=tt, hps=hps),
        grid=(batch, HEADS // hps, nt),
        in_specs=[
            pl.BlockSpec((2, wid), lambda b, h, t: (0, h)),
            pl.BlockSpec((1, HEAD), lambda b, h, t: (0, 0)),
            seg(SEG_AQ), seg(SEG_AF), seg(SEG_AI), seg(SEG_AG),
        ],
        out_specs=pl.BlockSpec((tt, wid), lambda b, h, t: (b * nt + t, h)),
        out_shape=jax.ShapeDtypeStruct((m, WIDTH), BF16),
        scratch_shapes=[pltpu.VMEM((hps, HEAD, HEAD), F32)],
        compiler_params=pltpu.CompilerParams(
            dimension_semantics=("arbitrary", "arbitrary", "arbitrary"),
            vmem_limit_bytes=VMEM_LIMIT),
        name="hgrn",
    )(a_lower_bound, a_out_norm, proj, proj, proj, proj)


def _qkv_prep_kernel(qg_ref, kg_ref, bq_ref, bk_ref, bv_ref, qt_ref, kn_ref, vt_ref):
    r = lax.broadcasted_iota(jnp.int32, (HEAD, HEAD), 0)
    c = lax.broadcasted_iota(jnp.int32, (HEAD, HEAD), 1)
    grp = ((r >> 6) == (c >> 6)).astype(BF16)

    def gnorm(x, gain):
        x2 = x * x
        hi = x2.astype(BF16)
        lo = (x2 - hi.astype(F32)).astype(BF16)
        ss = _dot(hi, grp, NN) + _dot(lo, grp, NN)
        return x * lax.rsqrt(ss * (1.0 / QK) + EPS) * gain

    tr = bq_ref.shape[0]
    pad_row = lax.broadcasted_iota(jnp.int32, (VT_ROWS - HEAD, tr), 0)
    ones_row = (pad_row == 0).astype(BF16)
    for h in range(HEADS):
        hs = slice(h * HEAD, (h + 1) * HEAD)
        qn = gnorm(bq_ref[:, hs].astype(F32), qg_ref[...]) * (QK ** -0.5 * LOG2E)
        qt_ref[0, h] = qn.T.astype(BF16)
        kn_ref[:, hs] = gnorm(bk_ref[:, hs].astype(F32), kg_ref[...]).astype(BF16)
        vt_ref[0, h, 0:HEAD, :] = bv_ref[:, hs].astype(F32).T.astype(BF16)
        vt_ref[0, h, HEAD:VT_ROWS, :] = ones_row


def _qkv_prep(proj, q_gain, k_gain, batch, seq, tr=512):
    m = proj.shape[0]
    nt = seq // tr

    def seg(s):
        return pl.BlockSpec((tr, WIDTH), lambda b, t, s=s: (b * nt + t, s // HEADS))

    def col_out(rows):
        return pl.BlockSpec((1, HEADS, rows, tr), lambda b, t: (b, 0, 0, t))

    return pl.pallas_call(
        _qkv_prep_kernel,
        grid=(batch, nt),
        in_specs=[
            pl.BlockSpec((1, HEAD), lambda b, t: (0, 0)),
            pl.BlockSpec((1, HEAD), lambda b, t: (0, 0)),
            seg(SEG_BQ), seg(SEG_BK), seg(SEG_BV),
        ],
        out_specs=[col_out(HEAD), pl.BlockSpec((tr, WIDTH), lambda b, t: (b * nt + t, 0)),
                   col_out(VT_ROWS)],
        out_shape=[jax.ShapeDtypeStruct((batch, HEADS, HEAD, seq), BF16),
                   jax.ShapeDtypeStruct((m, WIDTH), BF16),
                   jax.ShapeDtypeStruct((batch, HEADS, VT_ROWS, seq), BF16)],
        compiler_params=pltpu.CompilerParams(
            dimension_semantics=("arbitrary", "arbitrary"), vmem_limit_bytes=VMEM_LIMIT),
        name="qkv_prep",
    )(q_gain, k_gain, proj, proj, proj)


def _attn_kernel(lam_ref, sub_ref, qt_ref, k_ref, vt_ref, g_ref, o_ref,
                 sa0, sa1, pb0, pb1, acc_ref, qx_ref, dg_ref, *, seq, tk):
    sa = (sa0, sa1)
    pb = (pb0, pb1)
    ncb = seq // tk
    nslot = ncb // 2
    h = pl.program_id(1)
    slope = jnp.exp2(-(jnp.zeros((1, 1), F32) + (h + 1).astype(F32))) * LOG2E
    c_hi = slope.astype(BF16).astype(F32)
    c_lo = (slope - c_hi).astype(BF16).astype(F32)
    slope = c_hi + c_lo

    s_i = lax.broadcasted_iota(jnp.int32, (tk, tk), 0)
    t_i = lax.broadcasted_iota(jnp.int32, (tk, tk), 1)
    corr = slope * jnp.where(s_i <= t_i, 0, 2 * (t_i - s_i)).astype(F32)
    dg_ref[...] = jnp.where((s_i >> 6) <= (t_i >> 6), corr, NEG)

    rowd = lax.broadcasted_iota(jnp.int32, (HEAD, tk), 0)
    auxq = jnp.where(rowd == 0, c_hi, jnp.where(rowd == 1, c_lo, 0.0)).astype(BF16)
    for cb in range(ncb):
        cs = slice(cb * tk, (cb + 1) * tk)
        qs = qt_ref[0, 0, :, cs]
        zero = jnp.zeros_like(qs)
        qx_ref[0, 0:HEAD, cs] = jnp.where(rowd < QK, qs, zero)
        qx_ref[1, 0:HEAD, cs] = jnp.where(rowd >= QK, qs, zero)
        qx_ref[0, HEAD:2 * HEAD, cs] = auxq
        qx_ref[1, HEAD:2 * HEAD, cs] = auxq
    lane_k = lax.broadcasted_iota(jnp.int32, (tk, HEAD), 1)
    row_k = lax.broadcasted_iota(jnp.int32, (tk, HEAD), 0)
    auxk = jnp.where(lane_k <= 1, row_k, 0).astype(F32).astype(BF16)

    lv = lam_ref[...]
    lam = (jnp.exp(jnp.sum(lv[0:1, :] * lv[1:2, :], keepdims=True))
           - jnp.exp(jnp.sum(lv[2:3, :] * lv[3:4, :], keepdims=True)) + LAMBDA_INIT)

    def phase_units(p):
        if not 0 <= p <= ncb:
            return []
        return [(s, s, p) if p <= s else (s, ncb - 1 - s, p - s - 1) for s in range(nslot)]

    m_run, alpha = {}, {}

    def finish(cb):
        cs = slice(cb * tk, (cb + 1) * tk)
        ot = (acc_ref[0, 0:HEAD, cs] / acc_ref[0, HEAD:HEAD + 1, cs]
              - lam * (acc_ref[1, 0:HEAD, cs] / acc_ref[1, HEAD:HEAD + 1, cs]))
        o = ot.T
        ms = jnp.mean(o * o, axis=-1, keepdims=True)
        y = o * lax.rsqrt(ms + EPS) * sub_ref[...] * (1.0 - LAMBDA_INIT)
        g = g_ref[cs, :].astype(F32)
        o_ref[cs, :] = (y * (g * jax.nn.sigmoid(g))).astype(o_ref.dtype)

    for p in range(-1, ncb + 2):
        ua, ub, uc = phase_units(p + 1), phase_units(p), phase_units(p - 1)
        for i in range(nslot):
            for mp in range(2):
                if ua:
                    s, cb, d = ua[i]
                    kx = jnp.concatenate([k_ref[d * tk:(d + 1) * tk, :], auxk], axis=1)
                    sa[(p + 1) % 2][mp, :, s * tk:(s + 1) * tk] = _dot(
                        kx, qx_ref[mp, :, cb * tk:(cb + 1) * tk], NN)
                if ub:
                    s, cb, d = ub[i]
                    ss = slice(s * tk, (s + 1) * tk)
                    st = sa[p % 2][mp, :, ss]
                    if d == cb:
                        st = st + dg_ref[...]
                    mx = jnp.max(st, axis=0, keepdims=True) + slope * float((d - cb) * tk)
                    if d == 0:
                        m_new = mx
                    else:
                        m_new = jnp.maximum(m_run[mp, cb], mx)
                        alpha[mp, cb, d] = jnp.exp2(m_run[mp, cb] - m_new)
                    m_run[mp, cb] = m_new
                    shift = m_new - slope * float((d - cb) * tk)
                    pb[p % 2][mp, :, ss] = jnp.exp2(st - shift).astype(BF16)
                if uc:
                    s, cb, d = uc[i]
                    cs = slice(cb * tk, (cb + 1) * tk)
                    pv = _dot(vt_ref[0, 0, :, d * tk:(d + 1) * tk],
                              pb[(p - 1) % 2][mp, :, s * tk:(s + 1) * tk], NN)
                    acc_ref[mp, :, cs] = pv if d == 0 else alpha[mp, cb, d] * acc_ref[mp, :, cs] + pv
            if uc and uc[i][2] == uc[i][1]:
                finish(uc[i][1])


def _attn(qt, kn, vt, proj, b_lambda, b_subln, batch, seq, tk=256):
    ncb = seq // tk
    assert seq % (2 * tk) == 0
    m = kn.shape[0]
    return pl.pallas_call(
        functools.partial(_attn_kernel, seq=seq, tk=tk),
        grid=(batch, HEADS),
        in_specs=[
            pl.BlockSpec((4, QK), lambda b, h: (0, 0)),
            pl.BlockSpec((1, HEAD), lambda b, h: (0, 0)),
            pl.BlockSpec((1, 1, HEAD, seq), lambda b, h: (b, h, 0, 0)),
            pl.BlockSpec((seq, HEAD), lambda b, h: (b, h)),
            pl.BlockSpec((1, 1, VT_ROWS, seq), lambda b, h: (b, h, 0, 0)),
            pl.BlockSpec((seq, HEAD), lambda b, h: (b, SEG_BG + h)),
        ],
        out_specs=pl.BlockSpec((seq, HEAD), lambda b, h: (b, h)),
        out_shape=jax.ShapeDtypeStruct((m, WIDTH), BF16),
        scratch_shapes=[
            pltpu.VMEM((2, tk, ncb // 2 * tk), F32), pltpu.VMEM((2, tk, ncb // 2 * tk), F32),
            pltpu.VMEM((2, tk, ncb // 2 * tk), BF16), pltpu.VMEM((2, tk, ncb // 2 * tk), BF16),
            pltpu.VMEM((2, VT_ROWS, seq), F32), pltpu.VMEM((2, 2 * HEAD, seq), BF16),
            pltpu.VMEM((tk, tk), F32),
        ],
        compiler_params=pltpu.CompilerParams(
            dimension_semantics=("arbitrary", "arbitrary"), vmem_limit_bytes=VMEM_LIMIT),
        name="attn",
    )(b_lambda, b_subln, qt, kn, vt, proj)


def _out_proj_kernel(ya_ref, yb_ref, ga_ref, gb_ref, x_ref, wa_ref, wb_ref, wo_ref, o_ref):
    ya = jnp.dot(ya_ref[...], wa_ref[...], preferred_element_type=F32)
    yb = jnp.dot(yb_ref[...], wb_ref[...], preferred_element_type=F32)
    mixed = (jax.nn.sigmoid(ga_ref[...].astype(F32)) * ya
             + jax.nn.sigmoid(gb_ref[...].astype(F32)) * yb)
    o_ref[...] = x_ref[...] + jnp.dot(mixed.astype(BF16), wo_ref[...],
                                      preferred_element_type=F32)


def _out_proj(ya, yb, proj, x2, wa, wb, wo, tm=512):
    m, d = x2.shape
    w = ya.shape[1]
    resident = functools.partial(pl.BlockSpec, pipeline_mode=pl.Buffered(1))
    return pl.pallas_call(
        _out_proj_kernel,
        grid=(m // tm,),
        in_specs=[
            pl.BlockSpec((tm, w), lambda i: (i, 0)),
            pl.BlockSpec((tm, w), lambda i: (i, 0)),
            pl.BlockSpec((tm, d), lambda i: (i, SEG_GATE_A)),
            pl.BlockSpec((tm, d), lambda i: (i, SEG_GATE_B)),
            pl.BlockSpec((tm, d), lambda i: (i, 0)),
            resident((w, d), lambda i: (0, 0)),
            resident((w, d), lambda i: (0, 0)),
            resident((d, d), lambda i: (0, 0)),
        ],
        out_specs=pl.BlockSpec((tm, d), lambda i: (i, 0)),
        out_shape=jax.ShapeDtypeStruct((m, d), F32),
        compiler_params=pltpu.CompilerParams(
            dimension_semantics=("arbitrary",), vmem_limit_bytes=VMEM_LIMIT),
        name="out_proj",
    )(ya, yb, proj, proj, x2, wa, wb, wo)


def kernel(x, norm_w, w_in, a_lower_bound, a_out_norm, b_q_norm, b_k_norm, b_lambda,
           b_subln, w_branch_a, w_branch_b, w_out):
    batch, seq, d = x.shape
    assert norm_w.shape[0] == 1 and a_lower_bound.shape == (2, WIDTH)
    assert d == 2 * WIDTH and seq % 512 == 0
    x2 = x.reshape(batch * seq, d)

    proj = _in_proj(x2, norm_w[0:1], w_in[0].astype(BF16))
    ya = _hgrn(proj, a_lower_bound, a_out_norm[0:1], batch, seq)
    qt, kn, vt = _qkv_prep(proj, jnp.tile(b_q_norm[0:1], (1, 2)),
                           jnp.tile(b_k_norm[0:1], (1, 2)), batch, seq)
    yb = _attn(qt, kn, vt, proj, b_lambda[0], b_subln[0:1], batch, seq)
    out = _out_proj(ya, yb, proj, x2, w_branch_a[0].astype(BF16),
                    w_branch_b[0].astype(BF16), w_out[0].astype(BF16))
    return out.reshape(batch, seq, d)
```

```python
import functools
import math

import jax
import jax.numpy as jnp
from jax import lax
from jax.experimental import pallas as pl
from jax.experimental.pallas import tpu as pltpu

F32 = jnp.float32
BF16 = jnp.bfloat16

EPS = 1e-6
CHUNK = 64
HEAD = 128
HEADS = 8
QK = 64
WIDTH = HEADS * HEAD
SUB = 8
CHUNK_SHIFT = CHUNK.bit_length() - 1
QK_SHIFT = QK.bit_length() - 1
assert CHUNK == 1 << CHUNK_SHIFT and QK == 1 << QK_SHIFT
LAMBDA_INIT = 0.8 - 0.6 * math.exp(-0.3 * 0)
NEG = -1e30
LOG2E = math.log2(math.e)
VT_ROWS = HEAD + 16

SEG_AQ, SEG_AF, SEG_AI, SEG_AG, SEG_BQ, SEG_BK, SEG_BV, SEG_BG = (s * HEADS for s in range(8))
SEG_GATE_A, SEG_GATE_B = 4, 5

VMEM_LIMIT = 56 * 1024 * 1024


def _dot(a, b, dims):
    return lax.dot_general(a, b, (dims, ((), ())), preferred_element_type=F32)


NN = ((1,), (0,))
NT = ((1,), (1,))
TN = ((0,), (0,))


def _in_proj_kernel(x_ref, nw_ref, w_ref, o_ref, h_ref):
    @pl.when(pl.program_id(1) == 0)
    def _():
        x = x_ref[...]
        ms = jnp.mean(x * x, axis=-1, keepdims=True)
        h_ref[...] = (x * lax.rsqrt(ms + EPS) * nw_ref[...]).astype(BF16)

    o_ref[...] = jnp.dot(h_ref[...], w_ref[...], preferred_element_type=F32).astype(o_ref.dtype)


def _in_proj(x2, norm_w, w_in_bf16, tm=1024, tn=2048):
    m, d = x2.shape
    n = w_in_bf16.shape[1]
    return pl.pallas_call(
        _in_proj_kernel,
        grid=(m // tm, n // tn),
        in_specs=[
            pl.BlockSpec((tm, d), lambda i, j: (i, 0)),
            pl.BlockSpec((1, d), lambda i, j: (0, 0)),
            pl.BlockSpec((d, tn), lambda i, j: (0, j)),
        ],
        out_specs=pl.BlockSpec((tm, tn), lambda i, j: (i, j)),
        out_shape=jax.ShapeDtypeStruct((m, n), BF16),
        scratch_shapes=[pltpu.VMEM((tm, d), BF16)],
        compiler_params=pltpu.CompilerParams(
            dimension_semantics=("arbitrary", "arbitrary"), vmem_limit_bytes=VMEM_LIMIT),
        name="in_proj",
    )(x2, norm_w, w_in_bf16)


def _hgrn_kernel(alb_ref, gain_ref, aq_ref, af_ref, ai_ref, ag_ref, o_ref, st_ref, *, tt, hps):
    @pl.when(pl.program_id(2) == 0)
    def _():
        st_ref[...] = jnp.zeros_like(st_ref)

    for hh in range(hps):
        _hgrn_head(alb_ref, gain_ref, aq_ref, af_ref, ai_ref, ag_ref, o_ref, st_ref, hh, tt)


def _hgrn_head(alb_ref, gain_ref, aq_ref, af_ref, ai_ref, ag_ref, o_ref, st_ref, hh, tt):
    hs = slice(hh * HEAD, (hh + 1) * HEAD)
    a = alb_ref[:, hs]
    a0, a1 = a[0:1, :], a[1:2, :]
    amax = jnp.maximum(a0, a1)
    e0, e1 = jnp.exp(a0 - amax), jnp.exp(a1 - amax)
    lb = e0 / (e0 + e1)
    one_m_lb = 1.0 - lb
    gain = gain_ref[...]

    nch = tt // CHUNK
    nb = CHUNK // SUB
    row = lax.broadcasted_iota(jnp.int32, (CHUNK, CHUNK), 0)
    col = lax.broadcasted_iota(jnp.int32, (CHUNK, CHUNK), 1)
    tril = (row >= col).astype(BF16)

    def pair_mask(blk):
        return (((row & (blk - 1)) >= blk // 2) & ((col & (blk - 1)) < blk // 2)
                & ((row & -blk) == (col & -blk)))

    rowt = lax.broadcasted_iota(jnp.int32, (tt, HEAD), 0)
    tl4 = lax.broadcasted_iota(jnp.int32, (1, nb, SUB, CHUNK), 2)
    rel4 = (lax.broadcasted_iota(jnp.int32, (1, nb, SUB, CHUNK), 3)
            - SUB * lax.broadcasted_iota(jnp.int32, (1, nb, SUB, CHUNK), 1))

    def rows(x, c):
        return x[c * CHUNK:(c + 1) * CHUNK]

    def row_bcast(x, step, off):
        return jnp.concatenate(
            [jnp.broadcast_to(x[j * step + off:j * step + off + 1, :], (step, HEAD))
             for j in range(tt // step)], axis=0)

    z = af_ref[:, hs].astype(F32)
    qa = aq_ref[:, hs].astype(F32)
    v = ai_ref[:, hs]
    g = ag_ref[:, hs].astype(F32)
    f = lb + one_m_lb * jax.nn.sigmoid(z)
    lf = jnp.log(f)
    k = 1.0 - f
    q = qa * jax.nn.sigmoid(qa)

    hi = lf.astype(BF16)
    lo = (lf - hi.astype(F32)).astype(BF16)
    b = jnp.concatenate(
        [_dot(tril, rows(hi, c), NN) + _dot(tril, rows(lo, c), NN) for c in range(nch)], axis=0)
    b_last = row_bcast(b, CHUNK, CHUNK - 1)
    qe = (q * jnp.exp(b)).astype(BF16)
    kd = (k * jnp.exp(b_last - b)).astype(BF16)
    dec = jnp.exp(b_last)

    def level(blk):
        half = blk // 2
        upper = (rowt & (blk - 1)) >= half
        d = b - row_bcast(b, blk, half)
        e = jnp.exp(jnp.where(upper, d, -d))
        xe = (jnp.where(upper, q, k) * e).astype(BF16)
        return [_dot(rows(xe, c), rows(xe, c), NT) for c in range(nch)]

    lv64, lv32, lv16 = level(64), level(32), level(16)
    m64, m32, m16 = pair_mask(64), pair_mask(32), pair_mask(16)

    shape4 = (nch, nb, SUB, HEAD)
    q4, f4 = q.reshape(shape4), f.reshape(shape4)
    gk = k.reshape(shape4)
    pt = jnp.zeros((nch, nb, SUB, CHUNK), F32)
    for j in range(SUB):
        if j:
            gk = f4 * pltpu.roll(gk, 1, axis=2)
        p = jnp.sum(q4 * gk, axis=3, keepdims=True)
        pt = jnp.where((rel4 == tl4 - j) & (tl4 >= j), p, pt)
    pt = pt.reshape(tt, CHUNK)

    intras = []
    for c in range(nch):
        sc = (jnp.where(m64, lv64[c], 0.0) + jnp.where(m32, lv32[c], 0.0)
              + jnp.where(m16, lv16[c], 0.0) + rows(pt, c))
        intras.append(_dot(sc.astype(BF16), rows(v, c), NN))
    updates = [_dot(rows(v, c), rows(kd, c), TN) for c in range(nch)]
    st = st_ref[hh]
    outs = []
    for c in range(nch):
        outs.append(intras[c] + _dot(rows(qe, c), st.astype(BF16), NT))
        st = st * dec[c * CHUNK:c * CHUNK + 1] + updates[c]
    st_ref[hh] = st

    out = jnp.concatenate(outs, axis=0)
    ms = jnp.mean(out * out, axis=-1, keepdims=True)
    y = out * lax.rsqrt(ms + EPS) * gain
    o_ref[:, hs] = (y * (g * jax.nn.sigmoid(g))).astype(o_ref.dtype)


def _hgrn(proj, a_lower_bound, a_out_norm, batch, seq, tt=512, hps=4):
    m = proj.shape[0]
    nt = seq // tt
    wid = hps * HEAD

    def seg(s):
        return pl.BlockSpec((tt, wid), lambda b, h, t, s=s: (b * nt + t, s // hps + h))

    return pl.pallas_call(
        functools.partial(_hgrn_kernel, tt=tt, hps=hps),
        grid=(batch, HEADS // hps, nt),
        in_specs=[
            pl.BlockSpec((2, wid), lambda b, h, t: (0, h)),
            pl.BlockSpec((1, HEAD), lambda b, h, t: (0, 0)),
            seg(SEG_AQ), seg(SEG_AF), seg(SEG_AI), seg(SEG_AG),
        ],
        out_specs=pl.BlockSpec((tt, wid), lambda b, h, t: (b * nt + t, h)),
        out_shape=jax.ShapeDtypeStruct((m, WIDTH), BF16),
        scratch_shapes=[pltpu.VMEM((hps, HEAD, HEAD), F32)],
        compiler_params=pltpu.CompilerParams(
            dimension_semantics=("arbitrary", "arbitrary", "arbitrary"),
            vmem_limit_bytes=VMEM_LIMIT),
        name="hgrn",
    )(a_lower_bound, a_out_norm, proj, proj, proj, proj)


def _qkv_prep_kernel(qg_ref, kg_ref, bq_ref, bk_ref, bv_ref, qt_ref, kn_ref, vt_ref):
    r = lax.broadcasted_iota(jnp.int32, (HEAD, HEAD), 0)
    c = lax.broadcasted_iota(jnp.int32, (HEAD, HEAD), 1)
    grp = ((r >> QK_SHIFT) == (c >> QK_SHIFT)).astype(BF16)

    def gnorm(x, gain):
        x2 = x * x
        hi = x2.astype(BF16)
        lo = (x2 - hi.astype(F32)).astype(BF16)
        ss = _dot(hi, grp, NN) + _dot(lo, grp, NN)
        return x * lax.rsqrt(ss * (1.0 / QK) + EPS) * gain

    tr = bq_ref.shape[0]
    pad_row = lax.broadcasted_iota(jnp.int32, (VT_ROWS - HEAD, tr), 0)
    ones_row = (pad_row == 0).astype(BF16)
    for h in range(HEADS):
        hs = slice(h * HEAD, (h + 1) * HEAD)
        qn = gnorm(bq_ref[:, hs].astype(F32), qg_ref[...]) * (QK ** -0.5 * LOG2E)
        qt_ref[0, h] = qn.T.astype(BF16)
        kn_ref[:, hs] = gnorm(bk_ref[:, hs].astype(F32), kg_ref[...]).astype(BF16)
        vt_ref[0, h, 0:HEAD, :] = bv_ref[:, hs].astype(F32).T.astype(BF16)
        vt_ref[0, h, HEAD:VT_ROWS, :] = ones_row


def _qkv_prep(proj, q_gain, k_gain, batch, seq, tr=512):
    m = proj.shape[0]
    nt = seq // tr

    def seg(s):
        return pl.BlockSpec((tr, WIDTH), lambda b, t, s=s: (b * nt + t, s // HEADS))

    def col_out(rows):
        return pl.BlockSpec((1, HEADS, rows, tr), lambda b, t: (b, 0, 0, t))

    return pl.pallas_call(
        _qkv_prep_kernel,
        grid=(batch, nt),
        in_specs=[
            pl.BlockSpec((1, HEAD), lambda b, t: (0, 0)),
            pl.BlockSpec((1, HEAD), lambda b, t: (0, 0)),
            seg(SEG_BQ), seg(SEG_BK), seg(SEG_BV),
        ],
        out_specs=[col_out(HEAD), pl.BlockSpec((tr, WIDTH), lambda b, t: (b * nt + t, 0)),
                   col_out(VT_ROWS)],
        out_shape=[jax.ShapeDtypeStruct((batch, HEADS, HEAD, seq), BF16),
                   jax.ShapeDtypeStruct((m, WIDTH), BF16),
                   jax.ShapeDtypeStruct((batch, HEADS, VT_ROWS, seq), BF16)],
        compiler_params=pltpu.CompilerParams(
            dimension_semantics=("arbitrary", "arbitrary"), vmem_limit_bytes=VMEM_LIMIT),
        name="qkv_prep",
    )(q_gain, k_gain, proj, proj, proj)


def _attn_kernel(lam_ref, sub_ref, qt_ref, k_ref, vt_ref, g_ref, o_ref,
                 sa0, sa1, pb0, pb1, acc_ref, qx_ref, dg_ref, *, seq, tk):
    sa = (sa0, sa1)
    pb = (pb0, pb1)
    ncb = seq // tk
    nslot = ncb // 2
    h = pl.program_id(1)
    slope = jnp.exp2(-(jnp.zeros((1, 1), F32) + (h + 1).astype(F32))) * LOG2E
    c_hi = slope.astype(BF16).astype(F32)
    c_lo = (slope - c_hi).astype(BF16).astype(F32)
    slope = c_hi + c_lo

    s_i = lax.broadcasted_iota(jnp.int32, (tk, tk), 0)
    t_i = lax.broadcasted_iota(jnp.int32, (tk, tk), 1)
    corr = slope * jnp.where(s_i <= t_i, 0, 2 * (t_i - s_i)).astype(F32)
    dg_ref[...] = jnp.where((s_i >> CHUNK_SHIFT) <= (t_i >> CHUNK_SHIFT), corr, NEG)

    rowd = lax.broadcasted_iota(jnp.int32, (HEAD, tk), 0)
    auxq = jnp.where(rowd == 0, c_hi, jnp.where(rowd == 1, c_lo, 0.0)).astype(BF16)
    for cb in range(ncb):
        cs = slice(cb * tk, (cb + 1) * tk)
        qs = qt_ref[0, 0, :, cs]
        zero = jnp.zeros_like(qs)
        qx_ref[0, 0:HEAD, cs] = jnp.where(rowd < QK, qs, zero)
        qx_ref[1, 0:HEAD, cs] = jnp.where(rowd >= QK, qs, zero)
        qx_ref[0, HEAD:2 * HEAD, cs] = auxq
        qx_ref[1, HEAD:2 * HEAD, cs] = auxq
    lane_k = lax.broadcasted_iota(jnp.int32, (tk, HEAD), 1)
    row_k = lax.broadcasted_iota(jnp.int32, (tk, HEAD), 0)
    auxk = jnp.where(lane_k <= 1, row_k, 0).astype(F32).astype(BF16)

    lv = lam_ref[...]
    lam = (jnp.exp(jnp.sum(lv[0:1, :] * lv[1:2, :], keepdims=True))
           - jnp.exp(jnp.sum(lv[2:3, :] * lv[3:4, :], keepdims=True)) + LAMBDA_INIT)

    def phase_units(p):
        if not 0 <= p <= ncb:
            return []
        return [(s, s, p) if p <= s else (s, ncb - 1 - s, p - s - 1) for s in range(nslot)]

    m_run, alpha = {}, {}

    def finish(cb):
        cs = slice(cb * tk, (cb + 1) * tk)
        inv0 = 1.0 / acc_ref[0, HEAD:HEAD + 1, cs]
        inv1 = lam / acc_ref[1, HEAD:HEAD + 1, cs]
        ot = acc_ref[0, 0:HEAD, cs] * inv0 - acc_ref[1, 0:HEAD, cs] * inv1
        o = ot.T
        ms = jnp.mean(o * o, axis=-1, keepdims=True)
        y = o * lax.rsqrt(ms + EPS) * sub_ref[...] * (1.0 - LAMBDA_INIT)
        g = g_ref[cs, :].astype(F32)
        o_ref[cs, :] = (y * (g * jax.nn.sigmoid(g))).astype(o_ref.dtype)

    for p in range(-1, ncb + 2):
        ua, ub, uc = phase_units(p + 1), phase_units(p), phase_units(p - 1)
        for i in range(nslot):
            for mp in range(2):
                if ua:
                    s, cb, d = ua[i]
                    kx = jnp.concatenate([k_ref[d * tk:(d + 1) * tk, :], auxk], axis=1)
                    sa[(p + 1) % 2][mp, :, s * tk:(s + 1) * tk] = _dot(
                        kx, qx_ref[mp, :, cb * tk:(cb + 1) * tk], NN)
                if ub:
                    s, cb, d = ub[i]
                    ss = slice(s * tk, (s + 1) * tk)
                    st = sa[p % 2][mp, :, ss]
                    if d == cb:
                        st = st + dg_ref[...]
                    mx = jnp.max(st, axis=0, keepdims=True) + slope * float((d - cb) * tk)
                    if d == 0:
                        m_new = mx
                    else:
                        m_new = jnp.maximum(m_run[mp, cb], mx)
                        alpha[mp, cb, d] = jnp.exp2(m_run[mp, cb] - m_new)
                    m_run[mp, cb] = m_new
                    shift = m_new - slope * float((d - cb) * tk)
                    pb[p % 2][mp, :, ss] = jnp.exp2(st - shift).astype(BF16)
                if uc:
                    s, cb, d = uc[i]
                    cs = slice(cb * tk, (cb + 1) * tk)
                    pv = _dot(vt_ref[0, 0, :, d * tk:(d + 1) * tk],
                              pb[(p - 1) % 2][mp, :, s * tk:(s + 1) * tk], NN)
                    acc_ref[mp, :, cs] = pv if d == 0 else alpha[mp, cb, d] * acc_ref[mp, :, cs] + pv
            if uc and uc[i][2] == uc[i][1]:
                finish(uc[i][1])


def _attn(qt, kn, vt, proj, b_lambda, b_subln, batch, seq, tk=256):
    ncb = seq // tk
    assert seq % (2 * tk) == 0
    m = kn.shape[0]
    return pl.pallas_call(
        functools.partial(_attn_kernel, seq=seq, tk=tk),
        grid=(batch, HEADS),
        in_specs=[
            pl.BlockSpec((4, QK), lambda b, h: (0, 0)),
            pl.BlockSpec((1, HEAD), lambda b, h: (0, 0)),
            pl.BlockSpec((1, 1, HEAD, seq), lambda b, h: (b, h, 0, 0)),
            pl.BlockSpec((seq, HEAD), lambda b, h: (b, h)),
            pl.BlockSpec((1, 1, VT_ROWS, seq), lambda b, h: (b, h, 0, 0)),
            pl.BlockSpec((seq, HEAD), lambda b, h: (b, SEG_BG + h)),
        ],
        out_specs=pl.BlockSpec((seq, HEAD), lambda b, h: (b, h)),
        out_shape=jax.ShapeDtypeStruct((m, WIDTH), BF16),
        scratch_shapes=[
            pltpu.VMEM((2, tk, ncb // 2 * tk), F32), pltpu.VMEM((2, tk, ncb // 2 * tk), F32),
            pltpu.VMEM((2, tk, ncb // 2 * tk), BF16), pltpu.VMEM((2, tk, ncb // 2 * tk), BF16),
            pltpu.VMEM((2, VT_ROWS, seq), F32), pltpu.VMEM((2, 2 * HEAD, seq), BF16),
            pltpu.VMEM((tk, tk), F32),
        ],
        compiler_params=pltpu.CompilerParams(
            dimension_semantics=("arbitrary", "arbitrary"), vmem_limit_bytes=VMEM_LIMIT),
        name="attn",
    )(b_lambda, b_subln, qt, kn, vt, proj)


def _out_proj_kernel(ya_ref, yb_ref, ga_ref, gb_ref, x_ref, wa_ref, wb_ref, wo_ref, o_ref):
    ya = jnp.dot(ya_ref[...], wa_ref[...], preferred_element_type=F32)
    yb = jnp.dot(yb_ref[...], wb_ref[...], preferred_element_type=F32)
    mixed = (jax.nn.sigmoid(ga_ref[...].astype(F32)) * ya
             + jax.nn.sigmoid(gb_ref[...].astype(F32)) * yb)
    o_ref[...] = x_ref[...] + jnp.dot(mixed.astype(BF16), wo_ref[...],
                                      preferred_element_type=F32)


def _out_proj(ya, yb, proj, x2, wa, wb, wo, tm=512):
    m, d = x2.shape
    w = ya.shape[1]
    resident = functools.partial(pl.BlockSpec, pipeline_mode=pl.Buffered(1))
    return pl.pallas_call(
        _out_proj_kernel,
        grid=(m // tm,),
        in_specs=[
            pl.BlockSpec((tm, w), lambda i: (i, 0)),
            pl.BlockSpec((tm, w), lambda i: (i, 0)),
            pl.BlockSpec((tm, d), lambda i: (i, SEG_GATE_A)),
            pl.BlockSpec((tm, d), lambda i: (i, SEG_GATE_B)),
            pl.BlockSpec((tm, d), lambda i: (i, 0)),
            resident((w, d), lambda i: (0, 0)),
            resident((w, d), lambda i: (0, 0)),
            resident((d, d), lambda i: (0, 0)),
        ],
        out_specs=pl.BlockSpec((tm, d), lambda i: (i, 0)),
        out_shape=jax.ShapeDtypeStruct((m, d), F32),
        compiler_params=pltpu.CompilerParams(
            dimension_semantics=("arbitrary",), vmem_limit_bytes=VMEM_LIMIT),
        name="out_proj",
    )(ya, yb, proj, proj, x2, wa, wb, wo)


def kernel(x, norm_w, w_in, a_lower_bound, a_out_norm, b_q_norm, b_k_norm, b_lambda,
           b_subln, w_branch_a, w_branch_b, w_out):
    batch, seq, d = x.shape
    assert norm_w.shape[0] == 1 and a_lower_bound.shape == (2, WIDTH)
    assert d == 2 * WIDTH and seq % 512 == 0
    x2 = x.reshape(batch * seq, d)

    proj = _in_proj(x2, norm_w[0:1], w_in[0].astype(BF16))
    ya = _hgrn(proj, a_lower_bound, a_out_norm[0:1], batch, seq)
    qt, kn, vt = _qkv_prep(proj, jnp.tile(b_q_norm[0:1], (1, 2)),
                           jnp.tile(b_k_norm[0:1], (1, 2)), batch, seq)
    yb = _attn(qt, kn, vt, proj, b_lambda[0], b_subln[0:1], batch, seq)
    out = _out_proj(ya, yb, proj, x2, w_branch_a[0].astype(BF16),
                    w_branch_b[0].astype(BF16), w_out[0].astype(BF16))
    return out.reshape(batch, seq, d)
```

```python
import functools
import math

import jax
import jax.numpy as jnp
from jax import lax
from jax.experimental import pallas as pl
from jax.experimental.pallas import tpu as pltpu

F32 = jnp.float32
BF16 = jnp.bfloat16

EPS = 1e-6
CHUNK = 64
HEAD = 128
HEADS = 8
QK = 64
WIDTH = HEADS * HEAD
SUB = 8
CHUNK_SHIFT = CHUNK.bit_length() - 1
QK_SHIFT = QK.bit_length() - 1
assert CHUNK == 1 << CHUNK_SHIFT and QK == 1 << QK_SHIFT
LAMBDA_INIT = 0.8 - 0.6 * math.exp(-0.3 * 0)
NEG = -1e30
LOG2E = math.log2(math.e)
VT_ROWS = HEAD + 16

SEG_AQ, SEG_AF, SEG_AI, SEG_AG, SEG_BQ, SEG_BK, SEG_BV, SEG_BG = (s * HEADS for s in range(8))
SEG_GATE_A, SEG_GATE_B = 4, 5

VMEM_LIMIT = 56 * 1024 * 1024


def _dot(a, b, dims):
    return lax.dot_general(a, b, (dims, ((), ())), preferred_element_type=F32)


NN = ((1,), (0,))
NT = ((1,), (1,))
TN = ((0,), (0,))


def _in_proj_kernel(x_ref, nw_ref, w_ref, o_ref, h_ref):
    @pl.when(pl.program_id(1) == 0)
    def _():
        x = x_ref[...]
        ms = jnp.mean(x * x, axis=-1, keepdims=True)
        h_ref[...] = (x * lax.rsqrt(ms + EPS) * nw_ref[...]).astype(BF16)

    o_ref[...] = jnp.dot(h_ref[...], w_ref[...], preferred_element_type=F32).astype(o_ref.dtype)


def _in_proj(x2, norm_w, w_in_bf16, tm=1024, tn=2048):
    m, d = x2.shape
    n = w_in_bf16.shape[1]
    return pl.pallas_call(
        _in_proj_kernel,
        grid=(m // tm, n // tn),
        in_specs=[
            pl.BlockSpec((tm, d), lambda i, j: (i, 0)),
            pl.BlockSpec((1, d), lambda i, j: (0, 0)),
            pl.BlockSpec((d, tn), lambda i, j: (0, j)),
        ],
        out_specs=pl.BlockSpec((tm, tn), lambda i, j: (i, j)),
        out_shape=jax.ShapeDtypeStruct((m, n), BF16),
        scratch_shapes=[pltpu.VMEM((tm, d), BF16)],
        compiler_params=pltpu.CompilerParams(
            dimension_semantics=("arbitrary", "arbitrary"), vmem_limit_bytes=VMEM_LIMIT),
        name="in_proj",
    )(x2, norm_w, w_in_bf16)


def _hgrn_kernel(alb_ref, gain_ref, aq_ref, af_ref, ai_ref, ag_ref, o_ref, st_ref, *, tt, hps):
    @pl.when(pl.program_id(2) == 0)
    def _():
        st_ref[...] = jnp.zeros_like(st_ref)

    for hh in range(hps):
        _hgrn_head(alb_ref, gain_ref, aq_ref, af_ref, ai_ref, ag_ref, o_ref, st_ref, hh, tt)


def _hgrn_head(alb_ref, gain_ref, aq_ref, af_ref, ai_ref, ag_ref, o_ref, st_ref, hh, tt):
    hs = slice(hh * HEAD, (hh + 1) * HEAD)
    a = alb_ref[:, hs]
    a0, a1 = a[0:1, :], a[1:2, :]
    amax = jnp.maximum(a0, a1)
    e0, e1 = jnp.exp(a0 - amax), jnp.exp(a1 - amax)
    lb = e0 / (e0 + e1)
    one_m_lb = 1.0 - lb
    gain = gain_ref[...]

    nch = tt // CHUNK
    nb = CHUNK // SUB
    row = lax.broadcasted_iota(jnp.int32, (CHUNK, CHUNK), 0)
    col = lax.broadcasted_iota(jnp.int32, (CHUNK, CHUNK), 1)
    tril = (row >= col).astype(BF16)

    def pair_mask(blk):
        return (((row & (blk - 1)) >= blk // 2) & ((col & (blk - 1)) < blk // 2)
                & ((row & -blk) == (col & -blk)))

    rowt = lax.broadcasted_iota(jnp.int32, (tt, HEAD), 0)
    tl4 = lax.broadcasted_iota(jnp.int32, (1, nb, SUB, CHUNK), 2)
    rel4 = (lax.broadcasted_iota(jnp.int32, (1, nb, SUB, CHUNK), 3)
            - SUB * lax.broadcasted_iota(jnp.int32, (1, nb, SUB, CHUNK), 1))

    def rows(x, c):
        return x[c * CHUNK:(c + 1) * CHUNK]

    def row_bcast(x, step, off):
        return jnp.concatenate(
            [jnp.broadcast_to(x[j * step + off:j * step + off + 1, :], (step, HEAD))
             for j in range(tt // step)], axis=0)

    z = af_ref[:, hs].astype(F32)
    qa = aq_ref[:, hs].astype(F32)
    v = ai_ref[:, hs]
    g = ag_ref[:, hs].astype(F32)
    f = lb + one_m_lb * jax.nn.sigmoid(z)
    lf = jnp.log(f)
    k = 1.0 - f
    q = qa * jax.nn.sigmoid(qa)

    hi = lf.astype(BF16)
    lo = (lf - hi.astype(F32)).astype(BF16)
    b = jnp.concatenate(
        [_dot(tril, rows(hi, c), NN) + _dot(tril, rows(lo, c), NN) for c in range(nch)], axis=0)
    b_last = row_bcast(b, CHUNK, CHUNK - 1)
    qe = (q * jnp.exp(b)).astype(BF16)
    kd = (k * jnp.exp(b_last - b)).astype(BF16)
    dec = jnp.exp(b_last)

    def level(blk):
        half = blk // 2
        upper = (rowt & (blk - 1)) >= half
        d = b - row_bcast(b, blk, half)
        e = jnp.exp(jnp.where(upper, d, -d))
        xe = (jnp.where(upper, q, k) * e).astype(BF16)
        return [_dot(rows(xe, c), rows(xe, c), NT) for c in range(nch)]

    lv64, lv32, lv16 = level(64), level(32), level(16)
    m64, m32, m16 = pair_mask(64), pair_mask(32), pair_mask(16)

    shape4 = (nch, nb, SUB, HEAD)
    q4, f4 = q.reshape(shape4), f.reshape(shape4)
    gk = k.reshape(shape4)
    pt = jnp.zeros((nch, nb, SUB, CHUNK), F32)
    for j in range(SUB):
        if j:
            gk = f4 * pltpu.roll(gk, 1, axis=2)
        p = jnp.sum(q4 * gk, axis=3, keepdims=True)
        pt = jnp.where((rel4 == tl4 - j) & (tl4 >= j), p, pt)
    pt = pt.reshape(tt, CHUNK)

    intras = []
    for c in range(nch):
        sc = (jnp.where(m64, lv64[c], 0.0) + jnp.where(m32, lv32[c], 0.0)
              + jnp.where(m16, lv16[c], 0.0) + rows(pt, c))
        intras.append(_dot(sc.astype(BF16), rows(v, c), NN))
    updates = [_dot(rows(v, c), rows(kd, c), TN) for c in range(nch)]
    st = st_ref[hh]
    outs = []
    for c in range(nch):
        outs.append(intras[c] + _dot(rows(qe, c), st.astype(BF16), NT))
        st = st * dec[c * CHUNK:c * CHUNK + 1] + updates[c]
    st_ref[hh] = st

    out = jnp.concatenate(outs, axis=0)
    ms = jnp.mean(out * out, axis=-1, keepdims=True)
    y = out * lax.rsqrt(ms + EPS) * gain
    o_ref[:, hs] = (y * (g * jax.nn.sigmoid(g))).astype(o_ref.dtype)


def _hgrn(proj, a_lower_bound, a_out_norm, batch, seq, tt=512, hps=4):
    m = proj.shape[0]
    nt = seq // tt
    wid = hps * HEAD

    def seg(s):
        return pl.BlockSpec((tt, wid), lambda b, h, t, s=s: (b * nt + t, s // hps + h))

    return pl.pallas_call(
        functools.partial(_hgrn_kernel, tt=tt, hps=hps),
        grid=(batch, HEADS // hps, nt),
        in_specs=[
            pl.BlockSpec((2, wid), lambda b, h, t: (0, h)),
            pl.BlockSpec((1, HEAD), lambda b, h, t: (0, 0)),
            seg(SEG_AQ), seg(SEG_AF), seg(SEG_AI), seg(SEG_AG),
        ],
        out_specs=pl.BlockSpec((tt, wid), lambda b, h, t: (b * nt + t, h)),
        out_shape=jax.ShapeDtypeStruct((m, WIDTH), BF16),
        scratch_shapes=[pltpu.VMEM((hps, HEAD, HEAD), F32)],
        compiler_params=pltpu.CompilerParams(
            dimension_semantics=("arbitrary", "arbitrary", "arbitrary"),
            vmem_limit_bytes=VMEM_LIMIT),
        name="hgrn",
    )(a_lower_bound, a_out_norm, proj, proj, proj, proj)


def _qkv_prep_kernel(qg_ref, kg_ref, bq_ref, bk_ref, bv_ref, qt_ref, kn_ref, vt_ref):
    r = lax.broadcasted_iota(jnp.int32, (HEAD, HEAD), 0)
    c = lax.broadcasted_iota(jnp.int32, (HEAD, HEAD), 1)
    grp = ((r >> QK_SHIFT) == (c >> QK_SHIFT)).astype(BF16)

    def gnorm(x, gain):
        ss = _dot((x * x).astype(BF16), grp, NN)
        return x * lax.rsqrt(ss * (1.0 / QK) + EPS) * gain

    tr = bq_ref.shape[0]
    pad_row = lax.broadcasted_iota(jnp.int32, (VT_ROWS - HEAD, tr), 0)
    ones_row = (pad_row == 0).astype(BF16)
    for h in range(HEADS):
        hs = slice(h * HEAD, (h + 1) * HEAD)
        qn = gnorm(bq_ref[:, hs].astype(F32), qg_ref[...]) * (QK ** -0.5 * LOG2E)
        qt_ref[0, h] = qn.T.astype(BF16)
        kn_ref[:, hs] = gnorm(bk_ref[:, hs].astype(F32), kg_ref[...]).astype(BF16)
        vt_ref[0, h, 0:HEAD, :] = bv_ref[:, hs].astype(F32).T.astype(BF16)
        vt_ref[0, h, HEAD:VT_ROWS, :] = ones_row


def _qkv_prep(proj, q_gain, k_gain, batch, seq, tr=512):
    m = proj.shape[0]
    nt = seq // tr

    def seg(s):
        return pl.BlockSpec((tr, WIDTH), lambda b, t, s=s: (b * nt + t, s // HEADS))

    def col_out(rows):
        return pl.BlockSpec((1, HEADS, rows, tr), lambda b, t: (b, 0, 0, t))

    return pl.pallas_call(
        _qkv_prep_kernel,
        grid=(batch, nt),
        in_specs=[
            pl.BlockSpec((1, HEAD), lambda b, t: (0, 0)),
            pl.BlockSpec((1, HEAD), lambda b, t: (0, 0)),
            seg(SEG_BQ), seg(SEG_BK), seg(SEG_BV),
        ],
        out_specs=[col_out(HEAD), pl.BlockSpec((tr, WIDTH), lambda b, t: (b * nt + t, 0)),
                   col_out(VT_ROWS)],
        out_shape=[jax.ShapeDtypeStruct((batch, HEADS, HEAD, seq), BF16),
                   jax.ShapeDtypeStruct((m, WIDTH), BF16),
                   jax.ShapeDtypeStruct((batch, HEADS, VT_ROWS, seq), BF16)],
        compiler_params=pltpu.CompilerParams(
            dimension_semantics=("arbitrary", "arbitrary"), vmem_limit_bytes=VMEM_LIMIT),
        name="qkv_prep",
    )(q_gain, k_gain, proj, proj, proj)


def _attn_kernel(lam_ref, sub_ref, qt_ref, k_ref, vt_ref, g_ref, o_ref,
                 sa0, sa1, pb0, pb1, acc_ref, qx_ref, dg_ref, *, seq, tk):
    sa = (sa0, sa1)
    pb = (pb0, pb1)
    ncb = seq // tk
    nslot = ncb // 2
    h = pl.program_id(1)
    slope = jnp.exp2(-(jnp.zeros((1, 1), F32) + (h + 1).astype(F32))) * LOG2E
    c_hi = slope.astype(BF16).astype(F32)
    c_lo = (slope - c_hi).astype(BF16).astype(F32)
    slope = c_hi + c_lo

    s_i = lax.broadcasted_iota(jnp.int32, (tk, tk), 0)
    t_i = lax.broadcasted_iota(jnp.int32, (tk, tk), 1)
    corr = slope * jnp.where(s_i <= t_i, 0, 2 * (t_i - s_i)).astype(F32)
    dg_ref[...] = jnp.where((s_i >> CHUNK_SHIFT) <= (t_i >> CHUNK_SHIFT), corr, NEG)

    rowd = lax.broadcasted_iota(jnp.int32, (HEAD, tk), 0)
    auxq = jnp.where(rowd == 0, c_hi, jnp.where(rowd == 1, c_lo, 0.0)).astype(BF16)
    for cb in range(ncb):
        cs = slice(cb * tk, (cb + 1) * tk)
        qs = qt_ref[0, 0, :, cs]
        zero = jnp.zeros_like(qs)
        qx_ref[0, 0:HEAD, cs] = jnp.where(rowd < QK, qs, zero)
        qx_ref[1, 0:HEAD, cs] = jnp.where(rowd >= QK, qs, zero)
        qx_ref[0, HEAD:2 * HEAD, cs] = auxq
        qx_ref[1, HEAD:2 * HEAD, cs] = auxq
    lane_k = lax.broadcasted_iota(jnp.int32, (tk, HEAD), 1)
    row_k = lax.broadcasted_iota(jnp.int32, (tk, HEAD), 0)
    auxk = jnp.where(lane_k <= 1, row_k, 0).astype(F32).astype(BF16)

    lv = lam_ref[...]
    lam = (jnp.exp(jnp.sum(lv[0:1, :] * lv[1:2, :], keepdims=True))
           - jnp.exp(jnp.sum(lv[2:3, :] * lv[3:4, :], keepdims=True)) + LAMBDA_INIT)

    def phase_units(p):
        if not 0 <= p <= ncb:
            return []
        return [(s, s, p) if p <= s else (s, ncb - 1 - s, p - s - 1) for s in range(nslot)]

    m_run, alpha = {}, {}

    def finish(cb):
        cs = slice(cb * tk, (cb + 1) * tk)
        inv0 = 1.0 / acc_ref[0, HEAD:HEAD + 1, cs]
        inv1 = lam / acc_ref[1, HEAD:HEAD + 1, cs]
        ot = acc_ref[0, 0:HEAD, cs] * inv0 - acc_ref[1, 0:HEAD, cs] * inv1
        o = ot.T
        ms = jnp.mean(o * o, axis=-1, keepdims=True)
        y = o * lax.rsqrt(ms + EPS) * sub_ref[...] * (1.0 - LAMBDA_INIT)
        g = g_ref[cs, :].astype(F32)
        o_ref[cs, :] = (y * (g * jax.nn.sigmoid(g))).astype(o_ref.dtype)

    for p in range(-1, ncb + 2):
        ua, ub, uc = phase_units(p + 1), phase_units(p), phase_units(p - 1)
        for i in range(nslot):
            for mp in range(2):
                if ua:
                    s, cb, d = ua[i]
                    kx = jnp.concatenate([k_ref[d * tk:(d + 1) * tk, :], auxk], axis=1)
                    sa[(p + 1) % 2][mp, :, s * tk:(s + 1) * tk] = _dot(
                        kx, qx_ref[mp, :, cb * tk:(cb + 1) * tk], NN)
                if ub:
                    s, cb, d = ub[i]
                    ss = slice(s * tk, (s + 1) * tk)
                    st = sa[p % 2][mp, :, ss]
                    if d == cb:
                        st = st + dg_ref[...]
                    mx = jnp.max(st, axis=0, keepdims=True) + slope * float((d - cb) * tk)
                    if d == 0:
                        m_new = mx
                    else:
                        m_new = jnp.maximum(m_run[mp, cb], mx)
                        alpha[mp, cb, d] = jnp.exp2(m_run[mp, cb] - m_new)
                    m_run[mp, cb] = m_new
                    shift = m_new - slope * float((d - cb) * tk)
                    pb[p % 2][mp, :, ss] = jnp.exp2(st - shift).astype(BF16)
                if uc:
                    s, cb, d = uc[i]
                    cs = slice(cb * tk, (cb + 1) * tk)
                    pv = _dot(vt_ref[0, 0, :, d * tk:(d + 1) * tk],
                              pb[(p - 1) % 2][mp, :, s * tk:(s + 1) * tk], NN)
                    acc_ref[mp, :, cs] = pv if d == 0 else alpha[mp, cb, d] * acc_ref[mp, :, cs] + pv
            if uc and uc[i][2] == uc[i][1]:
                finish(uc[i][1])


def _attn(qt, kn, vt, proj, b_lambda, b_subln, batch, seq, tk=256):
    ncb = seq // tk
    assert seq % (2 * tk) == 0
    m = kn.shape[0]
    return pl.pallas_call(
        functools.partial(_attn_kernel, seq=seq, tk=tk),
        grid=(batch, HEADS),
        in_specs=[
            pl.BlockSpec((4, QK), lambda b, h: (0, 0)),
            pl.BlockSpec((1, HEAD), lambda b, h: (0, 0)),
            pl.BlockSpec((1, 1, HEAD, seq), lambda b, h: (b, h, 0, 0)),
            pl.BlockSpec((seq, HEAD), lambda b, h: (b, h)),
            pl.BlockSpec((1, 1, VT_ROWS, seq), lambda b, h: (b, h, 0, 0)),
            pl.BlockSpec((seq, HEAD), lambda b, h: (b, SEG_BG + h)),
        ],
        out_specs=pl.BlockSpec((seq, HEAD), lambda b, h: (b, h)),
        out_shape=jax.ShapeDtypeStruct((m, WIDTH), BF16),
        scratch_shapes=[
            pltpu.VMEM((2, tk, ncb // 2 * tk), F32), pltpu.VMEM((2, tk, ncb // 2 * tk), F32),
            pltpu.VMEM((2, tk, ncb // 2 * tk), BF16), pltpu.VMEM((2, tk, ncb // 2 * tk), BF16),
            pltpu.VMEM((2, VT_ROWS, seq), F32), pltpu.VMEM((2, 2 * HEAD, seq), BF16),
            pltpu.VMEM((tk, tk), F32),
        ],
        compiler_params=pltpu.CompilerParams(
            dimension_semantics=("arbitrary", "arbitrary"), vmem_limit_bytes=VMEM_LIMIT),
        name="attn",
    )(b_lambda, b_subln, qt, kn, vt, proj)


def _out_proj_kernel(ya_ref, yb_ref, ga_ref, gb_ref, x_ref, wa_ref, wb_ref, wo_ref, o_ref):
    ya = jnp.dot(ya_ref[...], wa_ref[...], preferred_element_type=F32)
    yb = jnp.dot(yb_ref[...], wb_ref[...], preferred_element_type=F32)
    mixed = (jax.nn.sigmoid(ga_ref[...].astype(F32)) * ya
             + jax.nn.sigmoid(gb_ref[...].astype(F32)) * yb)
    o_ref[...] = x_ref[...] + jnp.dot(mixed.astype(BF16), wo_ref[...],
                                      preferred_element_type=F32)


def _out_proj(ya, yb, proj, x2, wa, wb, wo, tm=512):
    m, d = x2.shape
    w = ya.shape[1]
    resident = functools.partial(pl.BlockSpec, pipeline_mode=pl.Buffered(1))
    return pl.pallas_call(
        _out_proj_kernel,
        grid=(m // tm,),
        in_specs=[
            pl.BlockSpec((tm, w), lambda i: (i, 0)),
            pl.BlockSpec((tm, w), lambda i: (i, 0)),
            pl.BlockSpec((tm, d), lambda i: (i, SEG_GATE_A)),
            pl.BlockSpec((tm, d), lambda i: (i, SEG_GATE_B)),
            pl.BlockSpec((tm, d), lambda i: (i, 0)),
            resident((w, d), lambda i: (0, 0)),
            resident((w, d), lambda i: (0, 0)),
            resident((d, d), lambda i: (0, 0)),
        ],
        out_specs=pl.BlockSpec((tm, d), lambda i: (i, 0)),
        out_shape=jax.ShapeDtypeStruct((m, d), F32),
        compiler_params=pltpu.CompilerParams(
            dimension_semantics=("arbitrary",), vmem_limit_bytes=VMEM_LIMIT),
        name="out_proj",
    )(ya, yb, proj, proj, x2, wa, wb, wo)


def kernel(x, norm_w, w_in, a_lower_bound, a_out_norm, b_q_norm, b_k_norm, b_lambda,
           b_subln, w_branch_a, w_branch_b, w_out):
    batch, seq, d = x.shape
    assert norm_w.shape[0] == 1 and a_lower_bound.shape == (2, WIDTH)
    assert d == 2 * WIDTH and seq % 512 == 0
    x2 = x.reshape(batch * seq, d)

    proj = _in_proj(x2, norm_w[0:1], w_in[0].astype(BF16))
    ya = _hgrn(proj, a_lower_bound, a_out_norm[0:1], batch, seq)
    qt, kn, vt = _qkv_prep(proj, jnp.tile(b_q_norm[0:1], (1, 2)),
                           jnp.tile(b_k_norm[0:1], (1, 2)), batch, seq)
    yb = _attn(qt, kn, vt, proj, b_lambda[0], b_subln[0:1], batch, seq)
    out = _out_proj(ya, yb, proj, x2, w_branch_a[0].astype(BF16),
                    w_branch_b[0].astype(BF16), w_out[0].astype(BF16))
    return out.reshape(batch, seq, d)
```

```python
import functools
import math

import jax
import jax.numpy as jnp
from jax import lax
from jax.experimental import pallas as pl
from jax.experimental.pallas import tpu as pltpu

F32 = jnp.float32
BF16 = jnp.bfloat16

EPS = 1e-6
CHUNK = 64
HEAD = 128
HEADS = 8
QK = 64
WIDTH = HEADS * HEAD
SUB = 8
CHUNK_SHIFT = CHUNK.bit_length() - 1
QK_SHIFT = QK.bit_length() - 1
assert CHUNK == 1 << CHUNK_SHIFT and QK == 1 << QK_SHIFT
LAMBDA_INIT = 0.8 - 0.6 * math.exp(-0.3 * 0)
NEG = -1e30
LOG2E = math.log2(math.e)
VT_ROWS = HEAD + 16

SEG_AQ, SEG_AF, SEG_AI, SEG_AG, SEG_BQ, SEG_BK, SEG_BV, SEG_BG = (s * HEADS for s in range(8))
SEG_GATE_A, SEG_GATE_B = 4, 5

VMEM_LIMIT = 56 * 1024 * 1024
VMEM_LIMIT_IN_PROJ = 60 * 1024 * 1024


def _dot(a, b, dims):
    return lax.dot_general(a, b, (dims, ((), ())), preferred_element_type=F32)


NN = ((1,), (0,))
NT = ((1,), (1,))
TN = ((0,), (0,))


def _in_proj_kernel(x_ref, nw_ref, w_ref, o_ref, h_ref):
    @pl.when(pl.program_id(1) == 0)
    def _():
        x = x_ref[...]
        ms = jnp.mean(x * x, axis=-1, keepdims=True)
        h_ref[...] = (x * lax.rsqrt(ms + EPS) * nw_ref[...]).astype(BF16)

    o_ref[...] = jnp.dot(h_ref[...], w_ref[...], preferred_element_type=F32).astype(o_ref.dtype)


def _in_proj(x2, norm_w, w_in_bf16, tm=1024, tn=3072):
    m, d = x2.shape
    n = w_in_bf16.shape[1]
    return pl.pallas_call(
        _in_proj_kernel,
        grid=(m // tm, n // tn),
        in_specs=[
            pl.BlockSpec((tm, d), lambda i, j: (i, 0)),
            pl.BlockSpec((1, d), lambda i, j: (0, 0)),
            pl.BlockSpec((d, tn), lambda i, j: (0, j)),
        ],
        out_specs=pl.BlockSpec((tm, tn), lambda i, j: (i, j)),
        out_shape=jax.ShapeDtypeStruct((m, n), BF16),
        scratch_shapes=[pltpu.VMEM((tm, d), BF16)],
        compiler_params=pltpu.CompilerParams(
            dimension_semantics=("arbitrary", "arbitrary"),
            vmem_limit_bytes=VMEM_LIMIT_IN_PROJ),
        name="in_proj",
    )(x2, norm_w, w_in_bf16)


def _hgrn_kernel(alb_ref, gain_ref, aq_ref, af_ref, ai_ref, ag_ref, o_ref, st_ref, *, tt, hps):
    @pl.when(pl.program_id(2) == 0)
    def _():
        st_ref[...] = jnp.zeros_like(st_ref)

    for hh in range(hps):
        _hgrn_head(alb_ref, gain_ref, aq_ref, af_ref, ai_ref, ag_ref, o_ref, st_ref, hh, tt)


def _hgrn_head(alb_ref, gain_ref, aq_ref, af_ref, ai_ref, ag_ref, o_ref, st_ref, hh, tt):
    hs = slice(hh * HEAD, (hh + 1) * HEAD)
    a = alb_ref[:, hs]
    a0, a1 = a[0:1, :], a[1:2, :]
    amax = jnp.maximum(a0, a1)
    e0, e1 = jnp.exp(a0 - amax), jnp.exp(a1 - amax)
    lb = e0 / (e0 + e1)
    one_m_lb = 1.0 - lb
    gain = gain_ref[...]

    nch = tt // CHUNK
    nb = CHUNK // SUB
    row = lax.broadcasted_iota(jnp.int32, (CHUNK, CHUNK), 0)
    col = lax.broadcasted_iota(jnp.int32, (CHUNK, CHUNK), 1)
    tril = (row >= col).astype(BF16)

    def pair_mask(blk):
        return (((row & (blk - 1)) >= blk // 2) & ((col & (blk - 1)) < blk // 2)
                & ((row & -blk) == (col & -blk)))

    rowt = lax.broadcasted_iota(jnp.int32, (tt, HEAD), 0)
    tl4 = lax.broadcasted_iota(jnp.int32, (1, nb, SUB, CHUNK), 2)
    rel4 = (lax.broadcasted_iota(jnp.int32, (1, nb, SUB, CHUNK), 3)
            - SUB * lax.broadcasted_iota(jnp.int32, (1, nb, SUB, CHUNK), 1))

    def rows(x, c):
        return x[c * CHUNK:(c + 1) * CHUNK]

    def row_bcast(x, step, off):
        return jnp.concatenate(
            [jnp.broadcast_to(x[j * step + off:j * step + off + 1, :], (step, HEAD))
             for j in range(tt // step)], axis=0)

    z = af_ref[:, hs].astype(F32)
    qa = aq_ref[:, hs].astype(F32)
    v = ai_ref[:, hs]
    g = ag_ref[:, hs].astype(F32)
    f = lb + one_m_lb * jax.nn.sigmoid(z)
    lf = jnp.log(f)
    k = 1.0 - f
    q = qa * jax.nn.sigmoid(qa)

    hi = lf.astype(BF16)
    lo = (lf - hi.astype(F32)).astype(BF16)
    b = jnp.concatenate(
        [_dot(tril, rows(hi, c), NN) + _dot(tril, rows(lo, c), NN) for c in range(nch)], axis=0)
    b_last = row_bcast(b, CHUNK, CHUNK - 1)
    qe = (q * jnp.exp(b)).astype(BF16)
    kd = (k * jnp.exp(b_last - b)).astype(BF16)
    dec = jnp.exp(b_last)

    def level(blk):
        half = blk // 2
        upper = (rowt & (blk - 1)) >= half
        d = b - row_bcast(b, blk, half)
        e = jnp.exp(jnp.where(upper, d, -d))
        xe = (jnp.where(upper, q, k) * e).astype(BF16)
        return [_dot(rows(xe, c), rows(xe, c), NT) for c in range(nch)]

    lv64, lv32, lv16 = level(64), level(32), level(16)
    m64, m32, m16 = pair_mask(64), pair_mask(32), pair_mask(16)

    shape4 = (nch, nb, SUB, HEAD)
    q4, f4 = q.reshape(shape4), f.reshape(shape4)
    gk = k.reshape(shape4)
    pt = jnp.zeros((nch, nb, SUB, CHUNK), F32)
    for j in range(SUB):
        if j:
            gk = f4 * pltpu.roll(gk, 1, axis=2)
        p = jnp.sum(q4 * gk, axis=3, keepdims=True)
        pt = jnp.where((rel4 == tl4 - j) & (tl4 >= j), p, pt)
    pt = pt.reshape(tt, CHUNK)

    intras = []
    for c in range(nch):
        sc = (jnp.where(m64, lv64[c], 0.0) + jnp.where(m32, lv32[c], 0.0)
              + jnp.where(m16, lv16[c], 0.0) + rows(pt, c))
        intras.append(_dot(sc.astype(BF16), rows(v, c), NN))
    updates = [_dot(rows(v, c), rows(kd, c), TN) for c in range(nch)]
    st = st_ref[hh]
    outs = []
    for c in range(nch):
        outs.append(intras[c] + _dot(rows(qe, c), st.astype(BF16), NT))
        st = st * dec[c * CHUNK:c * CHUNK + 1] + updates[c]
    st_ref[hh] = st

    out = jnp.concatenate(outs, axis=0)
    ms = jnp.mean(out * out, axis=-1, keepdims=True)
    y = out * lax.rsqrt(ms + EPS) * gain
    o_ref[:, hs] = (y * (g * jax.nn.sigmoid(g))).astype(o_ref.dtype)


def _hgrn(proj, a_lower_bound, a_out_norm, batch, seq, tt=512, hps=4):
    m = proj.shape[0]
    nt = seq // tt
    wid = hps * HEAD

    def seg(s):
        return pl.BlockSpec((tt, wid), lambda b, h, t, s=s: (b * nt + t, s // hps + h))

    return pl.pallas_call(
        functools.partial(_hgrn_kernel, tt=tt, hps=hps),
        grid=(batch, HEADS // hps, nt),
        in_specs=[
            pl.BlockSpec((2, wid), lambda b, h, t: (0, h)),
            pl.BlockSpec((1, HEAD), lambda b, h, t: (0, 0)),
            seg(SEG_AQ), seg(SEG_AF), seg(SEG_AI), seg(SEG_AG),
        ],
        out_specs=pl.BlockSpec((tt, wid), lambda b, h, t: (b * nt + t, h)),
        out_shape=jax.ShapeDtypeStruct((m, WIDTH), BF16),
        scratch_shapes=[pltpu.VMEM((hps, HEAD, HEAD), F32)],
        compiler_params=pltpu.CompilerParams(
            dimension_semantics=("arbitrary", "arbitrary", "arbitrary"),
            vmem_limit_bytes=VMEM_LIMIT),
        name="hgrn",
    )(a_lower_bound, a_out_norm, proj, proj, proj, proj)


def _qkv_prep_kernel(qg_ref, kg_ref, bq_ref, bk_ref, bv_ref, qt_ref, kn_ref, vt_ref):
    r = lax.broadcasted_iota(jnp.int32, (HEAD, HEAD), 0)
    c = lax.broadcasted_iota(jnp.int32, (HEAD, HEAD), 1)
    grp = ((r >> QK_SHIFT) == (c >> QK_SHIFT)).astype(BF16)

    def gnorm(x, gain):
        ss = _dot((x * x).astype(BF16), grp, NN)
        return x * lax.rsqrt(ss * (1.0 / QK) + EPS) * gain

    tr = bq_ref.shape[0]
    pad_row = lax.broadcasted_iota(jnp.int32, (VT_ROWS - HEAD, tr), 0)
    ones_row = (pad_row == 0).astype(BF16)
    for h in range(HEADS):
        hs = slice(h * HEAD, (h + 1) * HEAD)
        qn = gnorm(bq_ref[:, hs].astype(F32), qg_ref[...]) * (QK ** -0.5 * LOG2E)
        qt_ref[0, h] = qn.T.astype(BF16)
        kn_ref[:, hs] = gnorm(bk_ref[:, hs].astype(F32), kg_ref[...]).astype(BF16)
        vt_ref[0, h, 0:HEAD, :] = bv_ref[:, hs].astype(F32).T.astype(BF16)
        vt_ref[0, h, HEAD:VT_ROWS, :] = ones_row


def _qkv_prep(proj, q_gain, k_gain, batch, seq, tr=512):
    m = proj.shape[0]
    nt = seq // tr

    def seg(s):
        return pl.BlockSpec((tr, WIDTH), lambda b, t, s=s: (b * nt + t, s // HEADS))

    def col_out(rows):
        return pl.BlockSpec((1, HEADS, rows, tr), lambda b, t: (b, 0, 0, t))

    return pl.pallas_call(
        _qkv_prep_kernel,
        grid=(batch, nt),
        in_specs=[
            pl.BlockSpec((1, HEAD), lambda b, t: (0, 0)),
            pl.BlockSpec((1, HEAD), lambda b, t: (0, 0)),
            seg(SEG_BQ), seg(SEG_BK), seg(SEG_BV),
        ],
        out_specs=[col_out(HEAD), pl.BlockSpec((tr, WIDTH), lambda b, t: (b * nt + t, 0)),
                   col_out(VT_ROWS)],
        out_shape=[jax.ShapeDtypeStruct((batch, HEADS, HEAD, seq), BF16),
                   jax.ShapeDtypeStruct((m, WIDTH), BF16),
                   jax.ShapeDtypeStruct((batch, HEADS, VT_ROWS, seq), BF16)],
        compiler_params=pltpu.CompilerParams(
            dimension_semantics=("arbitrary", "arbitrary"), vmem_limit_bytes=VMEM_LIMIT),
        name="qkv_prep",
    )(q_gain, k_gain, proj, proj, proj)


def _attn_kernel(lam_ref, sub_ref, qt_ref, k_ref, vt_ref, g_ref, o_ref,
                 sa0, sa1, pb0, pb1, acc_ref, qx_ref, dg_ref, *, seq, tk):
    sa = (sa0, sa1)
    pb = (pb0, pb1)
    ncb = seq // tk
    nslot = ncb // 2
    h = pl.program_id(1)
    slope = jnp.exp2(-(jnp.zeros((1, 1), F32) + (h + 1).astype(F32))) * LOG2E
    c_hi = slope.astype(BF16).astype(F32)
    c_lo = (slope - c_hi).astype(BF16).astype(F32)
    slope = c_hi + c_lo

    s_i = lax.broadcasted_iota(jnp.int32, (tk, tk), 0)
    t_i = lax.broadcasted_iota(jnp.int32, (tk, tk), 1)
    corr = slope * jnp.where(s_i <= t_i, 0, 2 * (t_i - s_i)).astype(F32)
    dg_ref[...] = jnp.where((s_i >> CHUNK_SHIFT) <= (t_i >> CHUNK_SHIFT), corr, NEG)

    rowd = lax.broadcasted_iota(jnp.int32, (HEAD, tk), 0)
    auxq = jnp.where(rowd == 0, c_hi, jnp.where(rowd == 1, c_lo, 0.0)).astype(BF16)
    for cb in range(ncb):
        cs = slice(cb * tk, (cb + 1) * tk)
        qs = qt_ref[0, 0, :, cs]
        zero = jnp.zeros_like(qs)
        qx_ref[0, 0:HEAD, cs] = jnp.where(rowd < QK, qs, zero)
        qx_ref[1, 0:HEAD, cs] = jnp.where(rowd >= QK, qs, zero)
        qx_ref[0, HEAD:2 * HEAD, cs] = auxq
        qx_ref[1, HEAD:2 * HEAD, cs] = auxq
    lane_k = lax.broadcasted_iota(jnp.int32, (tk, HEAD), 1)
    row_k = lax.broadcasted_iota(jnp.int32, (tk, HEAD), 0)
    auxk = jnp.where(lane_k <= 1, row_k, 0).astype(F32).astype(BF16)

    lv = lam_ref[...]
    lam = (jnp.exp(jnp.sum(lv[0:1, :] * lv[1:2, :], keepdims=True))
           - jnp.exp(jnp.sum(lv[2:3, :] * lv[3:4, :], keepdims=True)) + LAMBDA_INIT)

    def phase_units(p):
        if not 0 <= p <= ncb:
            return []
        return [(s, s, p) if p <= s else (s, ncb - 1 - s, p - s - 1) for s in range(nslot)]

    m_run, alpha = {}, {}

    def finish(cb):
        cs = slice(cb * tk, (cb + 1) * tk)
        inv0 = 1.0 / acc_ref[0, HEAD:HEAD + 1, cs]
        inv1 = lam / acc_ref[1, HEAD:HEAD + 1, cs]
        ot = acc_ref[0, 0:HEAD, cs] * inv0 - acc_ref[1, 0:HEAD, cs] * inv1
        o = ot.T
        ms = jnp.mean(o * o, axis=-1, keepdims=True)
        y = o * lax.rsqrt(ms + EPS) * sub_ref[...] * (1.0 - LAMBDA_INIT)
        g = g_ref[cs, :].astype(F32)
        o_ref[cs, :] = (y * (g * jax.nn.sigmoid(g))).astype(o_ref.dtype)

    for p in range(-1, ncb + 2):
        ua, ub, uc = phase_units(p + 1), phase_units(p), phase_units(p - 1)
        for i in range(nslot):
            for mp in range(2):
                if ua:
                    s, cb, d = ua[i]
                    kx = jnp.concatenate([k_ref[d * tk:(d + 1) * tk, :], auxk], axis=1)
                    sa[(p + 1) % 2][mp, :, s * tk:(s + 1) * tk] = _dot(
                        kx, qx_ref[mp, :, cb * tk:(cb + 1) * tk], NN)
                if ub:
                    s, cb, d = ub[i]
                    ss = slice(s * tk, (s + 1) * tk)
                    st = sa[p % 2][mp, :, ss]
                    if d == cb:
                        st = st + dg_ref[...]
                    mx = jnp.max(st, axis=0, keepdims=True) + slope * float((d - cb) * tk)
                    if d == 0:
                        m_new = mx
                    else:
                        m_new = jnp.maximum(m_run[mp, cb], mx)
                        alpha[mp, cb, d] = jnp.exp2(m_run[mp, cb] - m_new)
                    m_run[mp, cb] = m_new
                    shift = m_new - slope * float((d - cb) * tk)
                    pb[p % 2][mp, :, ss] = jnp.exp2(st - shift).astype(BF16)
                if uc:
                    s, cb, d = uc[i]
                    cs = slice(cb * tk, (cb + 1) * tk)
                    pv = _dot(vt_ref[0, 0, :, d * tk:(d + 1) * tk],
                              pb[(p - 1) % 2][mp, :, s * tk:(s + 1) * tk], NN)
                    acc_ref[mp, :, cs] = pv if d == 0 else alpha[mp, cb, d] * acc_ref[mp, :, cs] + pv
            if uc and uc[i][2] == uc[i][1]:
                finish(uc[i][1])


def _attn(qt, kn, vt, proj, b_lambda, b_subln, batch, seq, tk=256):
    ncb = seq // tk
    assert seq % (2 * tk) == 0
    m = kn.shape[0]
    return pl.pallas_call(
        functools.partial(_attn_kernel, seq=seq, tk=tk),
        grid=(batch, HEADS),
        in_specs=[
            pl.BlockSpec((4, QK), lambda b, h: (0, 0)),
            pl.BlockSpec((1, HEAD), lambda b, h: (0, 0)),
            pl.BlockSpec((1, 1, HEAD, seq), lambda b, h: (b, h, 0, 0)),
            pl.BlockSpec((seq, HEAD), lambda b, h: (b, h)),
            pl.BlockSpec((1, 1, VT_ROWS, seq), lambda b, h: (b, h, 0, 0)),
            pl.BlockSpec((seq, HEAD), lambda b, h: (b, SEG_BG + h)),
        ],
        out_specs=pl.BlockSpec((seq, HEAD), lambda b, h: (b, h)),
        out_shape=jax.ShapeDtypeStruct((m, WIDTH), BF16),
        scratch_shapes=[
            pltpu.VMEM((2, tk, ncb // 2 * tk), F32), pltpu.VMEM((2, tk, ncb // 2 * tk), F32),
            pltpu.VMEM((2, tk, ncb // 2 * tk), BF16), pltpu.VMEM((2, tk, ncb // 2 * tk), BF16),
            pltpu.VMEM((2, VT_ROWS, seq), F32), pltpu.VMEM((2, 2 * HEAD, seq), BF16),
            pltpu.VMEM((tk, tk), F32),
        ],
        compiler_params=pltpu.CompilerParams(
            dimension_semantics=("arbitrary", "arbitrary"), vmem_limit_bytes=VMEM_LIMIT),
        name="attn",
    )(b_lambda, b_subln, qt, kn, vt, proj)


def _out_proj_kernel(ya_ref, yb_ref, ga_ref, gb_ref, x_ref, wa_ref, wb_ref, wo_ref, o_ref):
    ya = jnp.dot(ya_ref[...], wa_ref[...], preferred_element_type=F32)
    yb = jnp.dot(yb_ref[...], wb_ref[...], preferred_element_type=F32)
    mixed = (jax.nn.sigmoid(ga_ref[...].astype(F32)) * ya
             + jax.nn.sigmoid(gb_ref[...].astype(F32)) * yb)
    o_ref[...] = x_ref[...] + jnp.dot(mixed.astype(BF16), wo_ref[...],
                                      preferred_element_type=F32)


def _out_proj(ya, yb, proj, x2, wa, wb, wo, tm=512):
    m, d = x2.shape
    w = ya.shape[1]
    resident = functools.partial(pl.BlockSpec, pipeline_mode=pl.Buffered(1))
    return pl.pallas_call(
        _out_proj_kernel,
        grid=(m // tm,),
        in_specs=[
            pl.BlockSpec((tm, w), lambda i: (i, 0)),
            pl.BlockSpec((tm, w), lambda i: (i, 0)),
            pl.BlockSpec((tm, d), lambda i: (i, SEG_GATE_A)),
            pl.BlockSpec((tm, d), lambda i: (i, SEG_GATE_B)),
            pl.BlockSpec((tm, d), lambda i: (i, 0)),
            resident((w, d), lambda i: (0, 0)),
            resident((w, d), lambda i: (0, 0)),
            resident((d, d), lambda i: (0, 0)),
        ],
        out_specs=pl.BlockSpec((tm, d), lambda i: (i, 0)),
        out_shape=jax.ShapeDtypeStruct((m, d), F32),
        compiler_params=pltpu.CompilerParams(
            dimension_semantics=("arbitrary",), vmem_limit_bytes=VMEM_LIMIT),
        name="out_proj",
    )(ya, yb, proj, proj, x2, wa, wb, wo)


def kernel(x, norm_w, w_in, a_lower_bound, a_out_norm, b_q_norm, b_k_norm, b_lambda,
           b_subln, w_branch_a, w_branch_b, w_out):
    batch, seq, d = x.shape
    assert norm_w.shape[0] == 1 and a_lower_bound.shape == (2, WIDTH)
    assert d == 2 * WIDTH and seq % 512 == 0
    x2 = x.reshape(batch * seq, d)

    proj = _in_proj(x2, norm_w[0:1], w_in[0].astype(BF16))
    ya = _hgrn(proj, a_lower_bound, a_out_norm[0:1], batch, seq)
    qt, kn, vt = _qkv_prep(proj, jnp.tile(b_q_norm[0:1], (1, 2)),
                           jnp.tile(b_k_norm[0:1], (1, 2)), batch, seq)
    yb = _attn(qt, kn, vt, proj, b_lambda[0], b_subln[0:1], batch, seq)
    out = _out_proj(ya, yb, proj, x2, w_branch_a[0].astype(BF16),
                    w_branch_b[0].astype(BF16), w_out[0].astype(BF16))
    return out.reshape(batch, seq, d)
```

```python
import functools
import math

import jax
import jax.numpy as jnp
from jax import lax
from jax.experimental import pallas as pl
from jax.experimental.pallas import tpu as pltpu

F32 = jnp.float32
BF16 = jnp.bfloat16

EPS = 1e-6
CHUNK = 64
HEAD = 128
HEADS = 8
QK = 64
WIDTH = HEADS * HEAD
SUB = 8
CHUNK_SHIFT = CHUNK.bit_length() - 1
QK_SHIFT = QK.bit_length() - 1
assert CHUNK == 1 << CHUNK_SHIFT and QK == 1 << QK_SHIFT
LAMBDA_INIT = 0.8 - 0.6 * math.exp(-0.3 * 0)
NEG = -1e30
LOG2E = math.log2(math.e)
VT_ROWS = HEAD + 16

SEG_AQ, SEG_AF, SEG_AI, SEG_AG, SEG_BQ, SEG_BK, SEG_BV, SEG_BG = (s * HEADS for s in range(8))
SEG_GATE_A, SEG_GATE_B = 4, 5

VMEM_LIMIT = 56 * 1024 * 1024
VMEM_LIMIT_IN_PROJ = 60 * 1024 * 1024


def _dot(a, b, dims):
    return lax.dot_general(a, b, (dims, ((), ())), preferred_element_type=F32)


NN = ((1,), (0,))
NT = ((1,), (1,))
TN = ((0,), (0,))


def _in_proj_kernel(x_ref, nw_ref, w_ref, o_ref, h_ref):
    @pl.when(pl.program_id(1) == 0)
    def _():
        x = x_ref[...]
        ms = jnp.mean(x * x, axis=-1, keepdims=True)
        h_ref[...] = (x * lax.rsqrt(ms + EPS) * nw_ref[...]).astype(BF16)

    o_ref[...] = jnp.dot(h_ref[...], w_ref[...], preferred_element_type=F32).astype(o_ref.dtype)


def _in_proj(x2, norm_w, w_in_bf16, tm=1024, tn=3072):
    m, d = x2.shape
    n = w_in_bf16.shape[1]
    return pl.pallas_call(
        _in_proj_kernel,
        grid=(m // tm, n // tn),
        in_specs=[
            pl.BlockSpec((tm, d), lambda i, j: (i, 0)),
            pl.BlockSpec((1, d), lambda i, j: (0, 0)),
            pl.BlockSpec((d, tn), lambda i, j: (0, j)),
        ],
        out_specs=pl.BlockSpec((tm, tn), lambda i, j: (i, j)),
        out_shape=jax.ShapeDtypeStruct((m, n), BF16),
        scratch_shapes=[pltpu.VMEM((tm, d), BF16)],
        compiler_params=pltpu.CompilerParams(
            dimension_semantics=("arbitrary", "arbitrary"),
            vmem_limit_bytes=VMEM_LIMIT_IN_PROJ),
        name="in_proj",
    )(x2, norm_w, w_in_bf16)


def _hgrn_kernel(alb_ref, gain_ref, aq_ref, af_ref, ai_ref, ag_ref, o_ref, st_ref, *, tt, hps):
    @pl.when(pl.program_id(2) == 0)
    def _():
        st_ref[...] = jnp.zeros_like(st_ref)

    for hh in range(hps):
        _hgrn_head(alb_ref, gain_ref, aq_ref, af_ref, ai_ref, ag_ref, o_ref, st_ref, hh, tt)


def _hgrn_head(alb_ref, gain_ref, aq_ref, af_ref, ai_ref, ag_ref, o_ref, st_ref, hh, tt):
    hs = slice(hh * HEAD, (hh + 1) * HEAD)
    a = alb_ref[:, hs]
    a0, a1 = a[0:1, :], a[1:2, :]
    amax = jnp.maximum(a0, a1)
    e0, e1 = jnp.exp(a0 - amax), jnp.exp(a1 - amax)
    lb = e0 / (e0 + e1)
    one_m_lb = 1.0 - lb
    gain = gain_ref[...]

    nch = tt // CHUNK
    nb = CHUNK // SUB
    row = lax.broadcasted_iota(jnp.int32, (CHUNK, CHUNK), 0)
    col = lax.broadcasted_iota(jnp.int32, (CHUNK, CHUNK), 1)
    tril = (row >= col).astype(BF16)

    def pair_mask(blk):
        return (((row & (blk - 1)) >= blk // 2) & ((col & (blk - 1)) < blk // 2)
                & ((row & -blk) == (col & -blk)))

    rowt = lax.broadcasted_iota(jnp.int32, (tt, HEAD), 0)
    tl4 = lax.broadcasted_iota(jnp.int32, (1, nb, SUB, CHUNK), 2)
    rel4 = (lax.broadcasted_iota(jnp.int32, (1, nb, SUB, CHUNK), 3)
            - SUB * lax.broadcasted_iota(jnp.int32, (1, nb, SUB, CHUNK), 1))

    def rows(x, c):
        return x[c * CHUNK:(c + 1) * CHUNK]

    def row_bcast(x, step, off):
        return jnp.concatenate(
            [jnp.broadcast_to(x[j * step + off:j * step + off + 1, :], (step, HEAD))
             for j in range(tt // step)], axis=0)

    z = af_ref[:, hs].astype(F32)
    qa = aq_ref[:, hs].astype(F32)
    v = ai_ref[:, hs]
    g = ag_ref[:, hs].astype(F32)
    f = lb + one_m_lb * jax.nn.sigmoid(z)
    lf = jnp.log(f)
    k = 1.0 - f
    q = qa * jax.nn.sigmoid(qa)

    hi = lf.astype(BF16)
    lo = (lf - hi.astype(F32)).astype(BF16)
    b = jnp.concatenate(
        [_dot(tril, rows(hi, c), NN) + _dot(tril, rows(lo, c), NN) for c in range(nch)], axis=0)
    b_last = row_bcast(b, CHUNK, CHUNK - 1)
    qe = (q * jnp.exp(b)).astype(BF16)
    kd = (k * jnp.exp(b_last - b)).astype(BF16)
    dec = jnp.exp(b_last)

    def level(blk):
        half = blk // 2
        upper = (rowt & (blk - 1)) >= half
        d = b - row_bcast(b, blk, half)
        e = jnp.exp(jnp.where(upper, d, -d))
        xe = (jnp.where(upper, q, k) * e).astype(BF16)
        return [_dot(rows(xe, c), rows(xe, c), NT) for c in range(nch)]

    lv64, lv32, lv16 = level(64), level(32), level(16)
    m64, m32, m16 = pair_mask(64), pair_mask(32), pair_mask(16)

    shape4 = (nch, nb, SUB, HEAD)
    q4, f4 = q.reshape(shape4), f.reshape(shape4)
    gk = k.reshape(shape4)
    pt = jnp.zeros((nch, nb, SUB, CHUNK), F32)
    for j in range(SUB):
        if j:
            gk = f4 * pltpu.roll(gk, 1, axis=2)
        p = jnp.sum(q4 * gk, axis=3, keepdims=True)
        pt = jnp.where((rel4 == tl4 - j) & (tl4 >= j), p, pt)
    pt = pt.reshape(tt, CHUNK)

    intras = []
    for c in range(nch):
        sc = (jnp.where(m64, lv64[c], 0.0) + jnp.where(m32, lv32[c], 0.0)
              + jnp.where(m16, lv16[c], 0.0) + rows(pt, c))
        intras.append(_dot(sc.astype(BF16), rows(v, c), NN))
    updates = [_dot(rows(v, c), rows(kd, c), TN) for c in range(nch)]
    st = st_ref[hh]
    outs = []
    for c in range(nch):
        outs.append(intras[c] + _dot(rows(qe, c), st.astype(BF16), NT))
        st = st * dec[c * CHUNK:c * CHUNK + 1] + updates[c]
    st_ref[hh] = st

    out = jnp.concatenate(outs, axis=0)
    ms = jnp.mean(out * out, axis=-1, keepdims=True)
    y = out * lax.rsqrt(ms + EPS) * gain
    o_ref[:, hs] = (y * (g * jax.nn.sigmoid(g))).astype(o_ref.dtype)


def _hgrn(proj, a_lower_bound, a_out_norm, batch, seq, tt=512, hps=4):
    m = proj.shape[0]
    nt = seq // tt
    wid = hps * HEAD

    def seg(s):
        return pl.BlockSpec((tt, wid), lambda b, h, t, s=s: (b * nt + t, s // hps + h))

    return pl.pallas_call(
        functools.partial(_hgrn_kernel, tt=tt, hps=hps),
        grid=(batch, HEADS // hps, nt),
        in_specs=[
            pl.BlockSpec((2, wid), lambda b, h, t: (0, h)),
            pl.BlockSpec((1, HEAD), lambda b, h, t: (0, 0)),
            seg(SEG_AQ), seg(SEG_AF), seg(SEG_AI), seg(SEG_AG),
        ],
        out_specs=pl.BlockSpec((tt, wid), lambda b, h, t: (b * nt + t, h)),
        out_shape=jax.ShapeDtypeStruct((m, WIDTH), BF16),
        scratch_shapes=[pltpu.VMEM((hps, HEAD, HEAD), F32)],
        compiler_params=pltpu.CompilerParams(
            dimension_semantics=("arbitrary", "arbitrary", "arbitrary"),
            vmem_limit_bytes=VMEM_LIMIT),
        name="hgrn",
    )(a_lower_bound, a_out_norm, proj, proj, proj, proj)


def _qkv_prep_kernel(qg_ref, kg_ref, bq_ref, bk_ref, bv_ref, qt_ref, kn_ref, vt_ref):
    r = lax.broadcasted_iota(jnp.int32, (HEAD, HEAD), 0)
    c = lax.broadcasted_iota(jnp.int32, (HEAD, HEAD), 1)
    grp = ((r >> QK_SHIFT) == (c >> QK_SHIFT)).astype(BF16)

    def gnorm(x, gain):
        ss = _dot((x * x).astype(BF16), grp, NN)
        return x * lax.rsqrt(ss * (1.0 / QK) + EPS) * gain

    tr = bq_ref.shape[0]
    pad_row = lax.broadcasted_iota(jnp.int32, (VT_ROWS - HEAD, tr), 0)
    ones_row = (pad_row == 0).astype(BF16)
    for h in range(HEADS):
        hs = slice(h * HEAD, (h + 1) * HEAD)
        qn = gnorm(bq_ref[:, hs].astype(F32), qg_ref[...]) * (QK ** -0.5 * LOG2E)
        qt_ref[0, h] = qn.astype(BF16).T
        kn_ref[:, hs] = gnorm(bk_ref[:, hs].astype(F32), kg_ref[...]).astype(BF16)
        vt_ref[0, h, 0:HEAD, :] = bv_ref[:, hs].T
        vt_ref[0, h, HEAD:VT_ROWS, :] = ones_row


def _qkv_prep(proj, q_gain, k_gain, batch, seq, tr=512):
    m = proj.shape[0]
    nt = seq // tr

    def seg(s):
        return pl.BlockSpec((tr, WIDTH), lambda b, t, s=s: (b * nt + t, s // HEADS))

    def col_out(rows):
        return pl.BlockSpec((1, HEADS, rows, tr), lambda b, t: (b, 0, 0, t))

    return pl.pallas_call(
        _qkv_prep_kernel,
        grid=(batch, nt),
        in_specs=[
            pl.BlockSpec((1, HEAD), lambda b, t: (0, 0)),
            pl.BlockSpec((1, HEAD), lambda b, t: (0, 0)),
            seg(SEG_BQ), seg(SEG_BK), seg(SEG_BV),
        ],
        out_specs=[col_out(HEAD), pl.BlockSpec((tr, WIDTH), lambda b, t: (b * nt + t, 0)),
                   col_out(VT_ROWS)],
        out_shape=[jax.ShapeDtypeStruct((batch, HEADS, HEAD, seq), BF16),
                   jax.ShapeDtypeStruct((m, WIDTH), BF16),
                   jax.ShapeDtypeStruct((batch, HEADS, VT_ROWS, seq), BF16)],
        compiler_params=pltpu.CompilerParams(
            dimension_semantics=("arbitrary", "arbitrary"), vmem_limit_bytes=VMEM_LIMIT),
        name="qkv_prep",
    )(q_gain, k_gain, proj, proj, proj)


def _attn_kernel(lam_ref, sub_ref, qt_ref, k_ref, vt_ref, g_ref, o_ref,
                 sa0, sa1, pb0, pb1, acc_ref, qx_ref, dg_ref, *, seq, tk):
    sa = (sa0, sa1)
    pb = (pb0, pb1)
    ncb = seq // tk
    nslot = ncb // 2
    h = pl.program_id(1)
    slope = jnp.exp2(-(jnp.zeros((1, 1), F32) + (h + 1).astype(F32))) * LOG2E
    c_hi = slope.astype(BF16).astype(F32)
    c_lo = (slope - c_hi).astype(BF16).astype(F32)
    slope = c_hi + c_lo

    s_i = lax.broadcasted_iota(jnp.int32, (tk, tk), 0)
    t_i = lax.broadcasted_iota(jnp.int32, (tk, tk), 1)
    corr = slope * jnp.where(s_i <= t_i, 0, 2 * (t_i - s_i)).astype(F32)
    dg_ref[...] = jnp.where((s_i >> CHUNK_SHIFT) <= (t_i >> CHUNK_SHIFT), corr, NEG)

    rowd = lax.broadcasted_iota(jnp.int32, (HEAD, tk), 0)
    auxq = jnp.where(rowd == 0, c_hi, jnp.where(rowd == 1, c_lo, 0.0)).astype(BF16)
    for cb in range(ncb):
        cs = slice(cb * tk, (cb + 1) * tk)
        qs = qt_ref[0, 0, :, cs]
        zero = jnp.zeros_like(qs)
        qx_ref[0, 0:HEAD, cs] = jnp.where(rowd < QK, qs, zero)
        qx_ref[1, 0:HEAD, cs] = jnp.where(rowd >= QK, qs, zero)
        qx_ref[0, HEAD:2 * HEAD, cs] = auxq
        qx_ref[1, HEAD:2 * HEAD, cs] = auxq
    lane_k = lax.broadcasted_iota(jnp.int32, (tk, HEAD), 1)
    row_k = lax.broadcasted_iota(jnp.int32, (tk, HEAD), 0)
    auxk = jnp.where(lane_k <= 1, row_k, 0).astype(F32).astype(BF16)

    lv = lam_ref[...]
    lam = (jnp.exp(jnp.sum(lv[0:1, :] * lv[1:2, :], keepdims=True))
           - jnp.exp(jnp.sum(lv[2:3, :] * lv[3:4, :], keepdims=True)) + LAMBDA_INIT)

    def phase_units(p):
        if not 0 <= p <= ncb:
            return []
        return [(s, s, p) if p <= s else (s, ncb - 1 - s, p - s - 1) for s in range(nslot)]

    m_run, alpha = {}, {}

    def finish(cb):
        cs = slice(cb * tk, (cb + 1) * tk)
        inv0 = 1.0 / acc_ref[0, HEAD:HEAD + 1, cs]
        inv1 = lam / acc_ref[1, HEAD:HEAD + 1, cs]
        ot = acc_ref[0, 0:HEAD, cs] * inv0 - acc_ref[1, 0:HEAD, cs] * inv1
        o = ot.T
        ms = jnp.mean(o * o, axis=-1, keepdims=True)
        y = o * lax.rsqrt(ms + EPS) * sub_ref[...] * (1.0 - LAMBDA_INIT)
        g = g_ref[cs, :].astype(F32)
        o_ref[cs, :] = (y * (g * jax.nn.sigmoid(g))).astype(o_ref.dtype)

    for p in range(-1, ncb + 2):
        ua, ub, uc = phase_units(p + 1), phase_units(p), phase_units(p - 1)
        for i in range(nslot):
            for mp in range(2):
                if ua:
                    s, cb, d = ua[i]
                    kx = jnp.concatenate([k_ref[d * tk:(d + 1) * tk, :], auxk], axis=1)
                    sa[(p + 1) % 2][mp, :, s * tk:(s + 1) * tk] = _dot(
                        kx, qx_ref[mp, :, cb * tk:(cb + 1) * tk], NN)
                if ub:
                    s, cb, d = ub[i]
                    ss = slice(s * tk, (s + 1) * tk)
                    st = sa[p % 2][mp, :, ss]
                    if d == cb:
                        st = st + dg_ref[...]
                    mx = jnp.max(st, axis=0, keepdims=True) + slope * float((d - cb) * tk)
                    if d == 0:
                        m_new = mx
                    else:
                        m_new = jnp.maximum(m_run[mp, cb], mx)
                        alpha[mp, cb, d] = jnp.exp2(m_run[mp, cb] - m_new)
                    m_run[mp, cb] = m_new
                    shift = m_new - slope * float((d - cb) * tk)
                    pb[p % 2][mp, :, ss] = jnp.exp2(st - shift).astype(BF16)
                if uc:
                    s, cb, d = uc[i]
                    cs = slice(cb * tk, (cb + 1) * tk)
                    pv = _dot(vt_ref[0, 0, :, d * tk:(d + 1) * tk],
                              pb[(p - 1) % 2][mp, :, s * tk:(s + 1) * tk], NN)
                    acc_ref[mp, :, cs] = pv if d == 0 else alpha[mp, cb, d] * acc_ref[mp, :, cs] + pv
            if uc and uc[i][2] == uc[i][1]:
                finish(uc[i][1])


def _attn(qt, kn, vt, proj, b_lambda, b_subln, batch, seq, tk=256):
    ncb = seq // tk
    assert seq % (2 * tk) == 0
    m = kn.shape[0]
    return pl.pallas_call(
        functools.partial(_attn_kernel, seq=seq, tk=tk),
        grid=(batch, HEADS),
        in_specs=[
            pl.BlockSpec((4, QK), lambda b, h: (0, 0)),
            pl.BlockSpec((1, HEAD), lambda b, h: (0, 0)),
            pl.BlockSpec((1, 1, HEAD, seq), lambda b, h: (b, h, 0, 0)),
            pl.BlockSpec((seq, HEAD), lambda b, h: (b, h)),
            pl.BlockSpec((1, 1, VT_ROWS, seq), lambda b, h: (b, h, 0, 0)),
            pl.BlockSpec((seq, HEAD), lambda b, h: (b, SEG_BG + h)),
        ],
        out_specs=pl.BlockSpec((seq, HEAD), lambda b, h: (b, h)),
        out_shape=jax.ShapeDtypeStruct((m, WIDTH), BF16),
        scratch_shapes=[
            pltpu.VMEM((2, tk, ncb // 2 * tk), F32), pltpu.VMEM((2, tk, ncb // 2 * tk), F32),
            pltpu.VMEM((2, tk, ncb // 2 * tk), BF16), pltpu.VMEM((2, tk, ncb // 2 * tk), BF16),
            pltpu.VMEM((2, VT_ROWS, seq), F32), pltpu.VMEM((2, 2 * HEAD, seq), BF16),
            pltpu.VMEM((tk, tk), F32),
        ],
        compiler_params=pltpu.CompilerParams(
            dimension_semantics=("arbitrary", "arbitrary"), vmem_limit_bytes=VMEM_LIMIT),
        name="attn",
    )(b_lambda, b_subln, qt, kn, vt, proj)


def _out_proj_kernel(ya_ref, yb_ref, ga_ref, gb_ref, x_ref, wa_ref, wb_ref, wo_ref, o_ref):
    ya = jnp.dot(ya_ref[...], wa_ref[...], preferred_element_type=F32)
    yb = jnp.dot(yb_ref[...], wb_ref[...], preferred_element_type=F32)
    mixed = (jax.nn.sigmoid(ga_ref[...].astype(F32)) * ya
             + jax.nn.sigmoid(gb_ref[...].astype(F32)) * yb)
    o_ref[...] = x_ref[...] + jnp.dot(mixed.astype(BF16), wo_ref[...],
                                      preferred_element_type=F32)


def _out_proj(ya, yb, proj, x2, wa, wb, wo, tm=512):
    m, d = x2.shape
    w = ya.shape[1]
    resident = functools.partial(pl.BlockSpec, pipeline_mode=pl.Buffered(1))
    return pl.pallas_call(
        _out_proj_kernel,
        grid=(m // tm,),
        in_specs=[
            pl.BlockSpec((tm, w), lambda i: (i, 0)),
            pl.BlockSpec((tm, w), lambda i: (i, 0)),
            pl.BlockSpec((tm, d), lambda i: (i, SEG_GATE_A)),
            pl.BlockSpec((tm, d), lambda i: (i, SEG_GATE_B)),
            pl.BlockSpec((tm, d), lambda i: (i, 0)),
            resident((w, d), lambda i: (0, 0)),
            resident((w, d), lambda i: (0, 0)),
            resident((d, d), lambda i: (0, 0)),
        ],
        out_specs=pl.BlockSpec((tm, d), lambda i: (i, 0)),
        out_shape=jax.ShapeDtypeStruct((m, d), F32),
        compiler_params=pltpu.CompilerParams(
            dimension_semantics=("arbitrary",), vmem_limit_bytes=VMEM_LIMIT),
        name="out_proj",
    )(ya, yb, proj, proj, x2, wa, wb, wo)


def kernel(x, norm_w, w_in, a_lower_bound, a_out_norm, b_q_norm, b_k_norm, b_lambda,
           b_subln, w_branch_a, w_branch_b, w_out):
    batch, seq, d = x.shape
    assert norm_w.shape[0] == 1 and a_lower_bound.shape == (2, WIDTH)
    assert d == 2 * WIDTH and seq % 512 == 0
    x2 = x.reshape(batch * seq, d)

    proj = _in_proj(x2, norm_w[0:1], w_in[0].astype(BF16))
    ya = _hgrn(proj, a_lower_bound, a_out_norm[0:1], batch, seq)
    qt, kn, vt = _qkv_prep(proj, jnp.tile(b_q_norm[0:1], (1, 2)),
                           jnp.tile(b_k_norm[0:1], (1, 2)), batch, seq)
    yb = _attn(qt, kn, vt, proj, b_lambda[0], b_subln[0:1], batch, seq)
    out = _out_proj(ya, yb, proj, x2, w_branch_a[0].astype(BF16),
                    w_branch_b[0].astype(BF16), w_out[0].astype(BF16))
    return out.reshape(batch, seq, d)
```

```python
import functools
import math

import jax
import jax.numpy as jnp
from jax import lax
from jax.experimental import pallas as pl
from jax.experimental.pallas import tpu as pltpu

F32 = jnp.float32
BF16 = jnp.bfloat16

EPS = 1e-6
CHUNK = 64
HEAD = 128
HEADS = 8
QK = 64
WIDTH = HEADS * HEAD
SUB = 8
CHUNK_SHIFT = CHUNK.bit_length() - 1
QK_SHIFT = QK.bit_length() - 1
assert CHUNK == 1 << CHUNK_SHIFT and QK == 1 << QK_SHIFT
LAMBDA_INIT = 0.8 - 0.6 * math.exp(-0.3 * 0)
NEG = -1e30
LOG2E = math.log2(math.e)
VT_ROWS = HEAD + 16

SEG_AQ, SEG_AF, SEG_AI, SEG_AG, SEG_BQ, SEG_BK, SEG_BV, SEG_BG = (s * HEADS for s in range(8))
SEG_GATE_A, SEG_GATE_B = 4, 5

VMEM_LIMIT = 56 * 1024 * 1024
VMEM_LIMIT_IN_PROJ = 60 * 1024 * 1024


def _dot(a, b, dims):
    return lax.dot_general(a, b, (dims, ((), ())), preferred_element_type=F32)


NN = ((1,), (0,))
NT = ((1,), (1,))
TN = ((0,), (0,))


def _in_proj_kernel(x_ref, nw_ref, w_ref, o_ref, h_ref, *, rc):
    j = pl.program_id(1)

    @pl.when(j == 0)
    def _():
        for r in range(x_ref.shape[0] // rc):
            rs = slice(r * rc, (r + 1) * rc)
            x = x_ref[rs, :]
            ms = jnp.mean(x * x, axis=-1, keepdims=True)
            h = (x * lax.rsqrt(ms + EPS) * nw_ref[...]).astype(BF16)
            h_ref[rs, :] = h
            o_ref[rs, :] = jnp.dot(h, w_ref[...], preferred_element_type=F32).astype(o_ref.dtype)

    @pl.when(j != 0)
    def _():
        o_ref[...] = jnp.dot(h_ref[...], w_ref[...],
                             preferred_element_type=F32).astype(o_ref.dtype)


def _in_proj(x2, norm_w, w_in_bf16, tm=1024, tn=3072, rc=256):
    m, d = x2.shape
    n = w_in_bf16.shape[1]
    return pl.pallas_call(
        functools.partial(_in_proj_kernel, rc=rc),
        grid=(m // tm, n // tn),
        in_specs=[
            pl.BlockSpec((tm, d), lambda i, j: (i, 0)),
            pl.BlockSpec((1, d), lambda i, j: (0, 0)),
            pl.BlockSpec((d, tn), lambda i, j: (0, j)),
        ],
        out_specs=pl.BlockSpec((tm, tn), lambda i, j: (i, j)),
        out_shape=jax.ShapeDtypeStruct((m, n), BF16),
        scratch_shapes=[pltpu.VMEM((tm, d), BF16)],
        compiler_params=pltpu.CompilerParams(
            dimension_semantics=("arbitrary", "arbitrary"),
            vmem_limit_bytes=VMEM_LIMIT_IN_PROJ),
        name="in_proj",
    )(x2, norm_w, w_in_bf16)


def _hgrn_kernel(alb_ref, gain_ref, aq_ref, af_ref, ai_ref, ag_ref, o_ref, st_ref, *, tt, hps):
    @pl.when(pl.program_id(2) == 0)
    def _():
        st_ref[...] = jnp.zeros_like(st_ref)

    for hh in range(hps):
        _hgrn_head(alb_ref, gain_ref, aq_ref, af_ref, ai_ref, ag_ref, o_ref, st_ref, hh, tt)


def _hgrn_head(alb_ref, gain_ref, aq_ref, af_ref, ai_ref, ag_ref, o_ref, st_ref, hh, tt):
    hs = slice(hh * HEAD, (hh + 1) * HEAD)
    a = alb_ref[:, hs]
    a0, a1 = a[0:1, :], a[1:2, :]
    amax = jnp.maximum(a0, a1)
    e0, e1 = jnp.exp(a0 - amax), jnp.exp(a1 - amax)
    lb = e0 / (e0 + e1)
    one_m_lb = 1.0 - lb
    gain = gain_ref[...]

    nch = tt // CHUNK
    nb = CHUNK // SUB
    row = lax.broadcasted_iota(jnp.int32, (CHUNK, CHUNK), 0)
    col = lax.broadcasted_iota(jnp.int32, (CHUNK, CHUNK), 1)
    tril = (row >= col).astype(BF16)

    def pair_mask(blk):
        return (((row & (blk - 1)) >= blk // 2) & ((col & (blk - 1)) < blk // 2)
                & ((row & -blk) == (col & -blk)))

    rowt = lax.broadcasted_iota(jnp.int32, (tt, HEAD), 0)
    tl4 = lax.broadcasted_iota(jnp.int32, (1, nb, SUB, CHUNK), 2)
    rel4 = (lax.broadcasted_iota(jnp.int32, (1, nb, SUB, CHUNK), 3)
            - SUB * lax.broadcasted_iota(jnp.int32, (1, nb, SUB, CHUNK), 1))

    def rows(x, c):
        return x[c * CHUNK:(c + 1) * CHUNK]

    def row_bcast(x, step, off):
        return jnp.concatenate(
            [jnp.broadcast_to(x[j * step + off:j * step + off + 1, :], (step, HEAD))
             for j in range(tt // step)], axis=0)

    z = af_ref[:, hs].astype(F32)
    qa = aq_ref[:, hs].astype(F32)
    v = ai_ref[:, hs]
    g = ag_ref[:, hs].astype(F32)
    f = lb + one_m_lb * jax.nn.sigmoid(z)
    lf = jnp.log(f)
    k = 1.0 - f
    q = qa * jax.nn.sigmoid(qa)

    hi = lf.astype(BF16)
    lo = (lf - hi.astype(F32)).astype(BF16)
    b = jnp.concatenate(
        [_dot(tril, rows(hi, c), NN) + _dot(tril, rows(lo, c), NN) for c in range(nch)], axis=0)
    b_last = row_bcast(b, CHUNK, CHUNK - 1)
    qe = (q * jnp.exp(b)).astype(BF16)
    kd = (k * jnp.exp(b_last - b)).astype(BF16)
    dec = jnp.exp(b_last)

    def level(blk):
        half = blk // 2
        upper = (rowt & (blk - 1)) >= half
        d = b - row_bcast(b, blk, half)
        e = jnp.exp(jnp.where(upper, d, -d))
        xe = (jnp.where(upper, q, k) * e).astype(BF16)
        return [_dot(rows(xe, c), rows(xe, c), NT) for c in range(nch)]

    lv64, lv32, lv16 = level(64), level(32), level(16)
    m64, m32, m16 = pair_mask(64), pair_mask(32), pair_mask(16)

    shape4 = (nch, nb, SUB, HEAD)
    q4, f4 = q.reshape(shape4), f.reshape(shape4)
    gk = k.reshape(shape4)
    pt = jnp.zeros((nch, nb, SUB, CHUNK), F32)
    for j in range(SUB):
        if j:
            gk = f4 * pltpu.roll(gk, 1, axis=2)
        p = jnp.sum(q4 * gk, axis=3, keepdims=True)
        pt = jnp.where((rel4 == tl4 - j) & (tl4 >= j), p, pt)
    pt = pt.reshape(tt, CHUNK)

    intras = []
    for c in range(nch):
        sc = (jnp.where(m64, lv64[c], 0.0) + jnp.where(m32, lv32[c], 0.0)
              + jnp.where(m16, lv16[c], 0.0) + rows(pt, c))
        intras.append(_dot(sc.astype(BF16), rows(v, c), NN))
    updates = [_dot(rows(v, c), rows(kd, c), TN) for c in range(nch)]
    st = st_ref[hh]
    outs = []
    for c in range(nch):
        outs.append(intras[c] + _dot(rows(qe, c), st.astype(BF16), NT))
        st = st * dec[c * CHUNK:c * CHUNK + 1] + updates[c]
    st_ref[hh] = st

    out = jnp.concatenate(outs, axis=0)
    ms = jnp.mean(out * out, axis=-1, keepdims=True)
    y = out * lax.rsqrt(ms + EPS) * gain
    o_ref[:, hs] = (y * (g * jax.nn.sigmoid(g))).astype(o_ref.dtype)


def _hgrn(proj, a_lower_bound, a_out_norm, batch, seq, tt=512, hps=4):
    m = proj.shape[0]
    nt = seq // tt
    wid = hps * HEAD

    def seg(s):
        return pl.BlockSpec((tt, wid), lambda b, h, t, s=s: (b * nt + t, s // hps + h))

    return pl.pallas_call(
        functools.partial(_hgrn_kernel, tt=tt, hps=hps),
        grid=(batch, HEADS // hps, nt),
        in_specs=[
            pl.BlockSpec((2, wid), lambda b, h, t: (0, h)),
            pl.BlockSpec((1, HEAD), lambda b, h, t: (0, 0)),
            seg(SEG_AQ), seg(SEG_AF), seg(SEG_AI), seg(SEG_AG),
        ],
        out_specs=pl.BlockSpec((tt, wid), lambda b, h, t: (b * nt + t, h)),
        out_shape=jax.ShapeDtypeStruct((m, WIDTH), BF16),
        scratch_shapes=[pltpu.VMEM((hps, HEAD, HEAD), F32)],
        compiler_params=pltpu.CompilerParams(
            dimension_semantics=("arbitrary", "arbitrary", "arbitrary"),
            vmem_limit_bytes=VMEM_LIMIT),
        name="hgrn",
    )(a_lower_bound, a_out_norm, proj, proj, proj, proj)


def _qkv_prep_kernel(qg_ref, kg_ref, bq_ref, bk_ref, bv_ref, qt_ref, kn_ref, vt_ref):
    r = lax.broadcasted_iota(jnp.int32, (HEAD, HEAD), 0)
    c = lax.broadcasted_iota(jnp.int32, (HEAD, HEAD), 1)
    grp = ((r >> QK_SHIFT) == (c >> QK_SHIFT)).astype(BF16)

    def gnorm(x, gain):
        ss = _dot((x * x).astype(BF16), grp, NN)
        return x * lax.rsqrt(ss * (1.0 / QK) + EPS) * gain

    tr = bq_ref.shape[0]
    pad_row = lax.broadcasted_iota(jnp.int32, (VT_ROWS - HEAD, tr), 0)
    ones_row = (pad_row == 0).astype(BF16)
    for h in range(HEADS):
        hs = slice(h * HEAD, (h + 1) * HEAD)
        qn = gnorm(bq_ref[:, hs].astype(F32), qg_ref[...]) * (QK ** -0.5 * LOG2E)
        qt_ref[0, h] = qn.T.astype(BF16)
        kn_ref[:, hs] = gnorm(bk_ref[:, hs].astype(F32), kg_ref[...]).astype(BF16)
        vt_ref[0, h, 0:HEAD, :] = bv_ref[:, hs].astype(F32).T.astype(BF16)
        vt_ref[0, h, HEAD:VT_ROWS, :] = ones_row


def _qkv_prep(proj, q_gain, k_gain, batch, seq, tr=512):
    m = proj.shape[0]
    nt = seq // tr

    def seg(s):
        return pl.BlockSpec((tr, WIDTH), lambda b, t, s=s: (b * nt + t, s // HEADS))

    def col_out(rows):
        return pl.BlockSpec((1, HEADS, rows, tr), lambda b, t: (b, 0, 0, t))

    return pl.pallas_call(
        _qkv_prep_kernel,
        grid=(batch, nt),
        in_specs=[
            pl.BlockSpec((1, HEAD), lambda b, t: (0, 0)),
            pl.BlockSpec((1, HEAD), lambda b, t: (0, 0)),
            seg(SEG_BQ), seg(SEG_BK), seg(SEG_BV),
        ],
        out_specs=[col_out(HEAD), pl.BlockSpec((tr, WIDTH), lambda b, t: (b * nt + t, 0)),
                   col_out(VT_ROWS)],
        out_shape=[jax.ShapeDtypeStruct((batch, HEADS, HEAD, seq), BF16),
                   jax.ShapeDtypeStruct((m, WIDTH), BF16),
                   jax.ShapeDtypeStruct((batch, HEADS, VT_ROWS, seq), BF16)],
        compiler_params=pltpu.CompilerParams(
            dimension_semantics=("arbitrary", "arbitrary"), vmem_limit_bytes=VMEM_LIMIT),
        name="qkv_prep",
    )(q_gain, k_gain, proj, proj, proj)


def _attn_kernel(lam_ref, sub_ref, qt_ref, k_ref, vt_ref, g_ref, o_ref,
                 sa0, sa1, pb0, pb1, acc_ref, qx_ref, dg_ref, *, seq, tk):
    sa = (sa0, sa1)
    pb = (pb0, pb1)
    ncb = seq // tk
    nslot = ncb // 2
    h = pl.program_id(1)
    slope = jnp.exp2(-(jnp.zeros((1, 1), F32) + (h + 1).astype(F32))) * LOG2E
    c_hi = slope.astype(BF16).astype(F32)
    c_lo = (slope - c_hi).astype(BF16).astype(F32)
    slope = c_hi + c_lo

    s_i = lax.broadcasted_iota(jnp.int32, (tk, tk), 0)
    t_i = lax.broadcasted_iota(jnp.int32, (tk, tk), 1)
    corr = slope * jnp.where(s_i <= t_i, 0, 2 * (t_i - s_i)).astype(F32)
    dg_ref[...] = jnp.where((s_i >> CHUNK_SHIFT) <= (t_i >> CHUNK_SHIFT), corr, NEG)

    rowd = lax.broadcasted_iota(jnp.int32, (HEAD, tk), 0)
    auxq = jnp.where(rowd == 0, c_hi, jnp.where(rowd == 1, c_lo, 0.0)).astype(BF16)
    for cb in range(ncb):
        cs = slice(cb * tk, (cb + 1) * tk)
        qs = qt_ref[0, 0, :, cs]
        zero = jnp.zeros_like(qs)
        qx_ref[0, 0:HEAD, cs] = jnp.where(rowd < QK, qs, zero)
        qx_ref[1, 0:HEAD, cs] = jnp.where(rowd >= QK, qs, zero)
        qx_ref[0, HEAD:2 * HEAD, cs] = auxq
        qx_ref[1, HEAD:2 * HEAD, cs] = auxq
    lane_k = lax.broadcasted_iota(jnp.int32, (tk, HEAD), 1)
    row_k = lax.broadcasted_iota(jnp.int32, (tk, HEAD), 0)
    auxk = jnp.where(lane_k <= 1, row_k, 0).astype(F32).astype(BF16)

    lv = lam_ref[...]
    lam = (jnp.exp(jnp.sum(lv[0:1, :] * lv[1:2, :], keepdims=True))
           - jnp.exp(jnp.sum(lv[2:3, :] * lv[3:4, :], keepdims=True)) + LAMBDA_INIT)

    def phase_units(p):
        if not 0 <= p <= ncb:
            return []
        return [(s, s, p) if p <= s else (s, ncb - 1 - s, p - s - 1) for s in range(nslot)]

    m_run, alpha = {}, {}

    def finish(cb):
        cs = slice(cb * tk, (cb + 1) * tk)
        inv0 = 1.0 / acc_ref[0, HEAD:HEAD + 1, cs]
        inv1 = lam / acc_ref[1, HEAD:HEAD + 1, cs]
        ot = acc_ref[0, 0:HEAD, cs] * inv0 - acc_ref[1, 0:HEAD, cs] * inv1
        o = ot.T
        ms = jnp.mean(o * o, axis=-1, keepdims=True)
        y = o * lax.rsqrt(ms + EPS) * sub_ref[...] * (1.0 - LAMBDA_INIT)
        g = g_ref[cs, :].astype(F32)
        o_ref[cs, :] = (y * (g * jax.nn.sigmoid(g))).astype(o_ref.dtype)

    for p in range(-1, ncb + 2):
        ua, ub, uc = phase_units(p + 1), phase_units(p), phase_units(p - 1)
        for i in range(nslot):
            for mp in range(2):
                if ua:
                    s, cb, d = ua[i]
                    kx = jnp.concatenate([k_ref[d * tk:(d + 1) * tk, :], auxk], axis=1)
                    sa[(p + 1) % 2][mp, :, s * tk:(s + 1) * tk] = _dot(
                        kx, qx_ref[mp, :, cb * tk:(cb + 1) * tk], NN)
                if ub:
                    s, cb, d = ub[i]
                    ss = slice(s * tk, (s + 1) * tk)
                    st = sa[p % 2][mp, :, ss]
                    if d == cb:
                        st = st + dg_ref[...]
                    mx = jnp.max(st, axis=0, keepdims=True) + slope * float((d - cb) * tk)
                    if d == 0:
                        m_new = mx
                    else:
                        m_new = jnp.maximum(m_run[mp, cb], mx)
                        alpha[mp, cb, d] = jnp.exp2(m_run[mp, cb] - m_new)
                    m_run[mp, cb] = m_new
                    shift = m_new - slope * float((d - cb) * tk)
                    pb[p % 2][mp, :, ss] = jnp.exp2(st - shift).astype(BF16)
                if uc:
                    s, cb, d = uc[i]
                    cs = slice(cb * tk, (cb + 1) * tk)
                    pv = _dot(vt_ref[0, 0, :, d * tk:(d + 1) * tk],
                              pb[(p - 1) % 2][mp, :, s * tk:(s + 1) * tk], NN)
                    acc_ref[mp, :, cs] = pv if d == 0 else alpha[mp, cb, d] * acc_ref[mp, :, cs] + pv
            if uc and uc[i][2] == uc[i][1]:
                finish(uc[i][1])


def _attn(qt, kn, vt, proj, b_lambda, b_subln, batch, seq, tk=256):
    ncb = seq // tk
    assert seq % (2 * tk) == 0
    m = kn.shape[0]
    return pl.pallas_call(
        functools.partial(_attn_kernel, seq=seq, tk=tk),
        grid=(batch, HEADS),
        in_specs=[
            pl.BlockSpec((4, QK), lambda b, h: (0, 0)),
            pl.BlockSpec((1, HEAD), lambda b, h: (0, 0)),
            pl.BlockSpec((1, 1, HEAD, seq), lambda b, h: (b, h, 0, 0)),
            pl.BlockSpec((seq, HEAD), lambda b, h: (b, h)),
            pl.BlockSpec((1, 1, VT_ROWS, seq), lambda b, h: (b, h, 0, 0)),
            pl.BlockSpec((seq, HEAD), lambda b, h: (b, SEG_BG + h)),
        ],
        out_specs=pl.BlockSpec((seq, HEAD), lambda b, h: (b, h)),
        out_shape=jax.ShapeDtypeStruct((m, WIDTH), BF16),
        scratch_shapes=[
            pltpu.VMEM((2, tk, ncb // 2 * tk), F32), pltpu.VMEM((2, tk, ncb // 2 * tk), F32),
            pltpu.VMEM((2, tk, ncb // 2 * tk), BF16), pltpu.VMEM((2, tk, ncb // 2 * tk), BF16),
            pltpu.VMEM((2, VT_ROWS, seq), F32), pltpu.VMEM((2, 2 * HEAD, seq), BF16),
            pltpu.VMEM((tk, tk), F32),
        ],
        compiler_params=pltpu.CompilerParams(
            dimension_semantics=("arbitrary", "arbitrary"), vmem_limit_bytes=VMEM_LIMIT),
        name="attn",
    )(b_lambda, b_subln, qt, kn, vt, proj)


def _out_proj_kernel(ya_ref, yb_ref, ga_ref, gb_ref, x_ref, wa_ref, wb_ref, wo_ref, o_ref):
    ya = jnp.dot(ya_ref[...], wa_ref[...], preferred_element_type=F32)
    yb = jnp.dot(yb_ref[...], wb_ref[...], preferred_element_type=F32)
    mixed = (jax.nn.sigmoid(ga_ref[...].astype(F32)) * ya
             + jax.nn.sigmoid(gb_ref[...].astype(F32)) * yb)
    o_ref[...] = x_ref[...] + jnp.dot(mixed.astype(BF16), wo_ref[...],
                                      preferred_element_type=F32)


def _out_proj(ya, yb, proj, x2, wa, wb, wo, tm=512):
    m, d = x2.shape
    w = ya.shape[1]
    resident = functools.partial(pl.BlockSpec, pipeline_mode=pl.Buffered(1))
    return pl.pallas_call(
        _out_proj_kernel,
        grid=(m // tm,),
        in_specs=[
            pl.BlockSpec((tm, w), lambda i: (i, 0)),
            pl.BlockSpec((tm, w), lambda i: (i, 0)),
            pl.BlockSpec((tm, d), lambda i: (i, SEG_GATE_A)),
            pl.BlockSpec((tm, d), lambda i: (i, SEG_GATE_B)),
            pl.BlockSpec((tm, d), lambda i: (i, 0)),
            resident((w, d), lambda i: (0, 0)),
            resident((w, d), lambda i: (0, 0)),
            resident((d, d), lambda i: (0, 0)),
        ],
        out_specs=pl.BlockSpec((tm, d), lambda i: (i, 0)),
        out_shape=jax.ShapeDtypeStruct((m, d), F32),
        compiler_params=pltpu.CompilerParams(
            dimension_semantics=("arbitrary",), vmem_limit_bytes=VMEM_LIMIT),
        name="out_proj",
    )(ya, yb, proj, proj, x2, wa, wb, wo)


def kernel(x, norm_w, w_in, a_lower_bound, a_out_norm, b_q_norm, b_k_norm, b_lambda,
           b_subln, w_branch_a, w_branch_b, w_out):
    batch, seq, d = x.shape
    assert norm_w.shape[0] == 1 and a_lower_bound.shape == (2, WIDTH)
    assert d == 2 * WIDTH and seq % 512 == 0
    x2 = x.reshape(batch * seq, d)

    proj = _in_proj(x2, norm_w[0:1], w_in[0].astype(BF16))
    ya = _hgrn(proj, a_lower_bound, a_out_norm[0:1], batch, seq)
    qt, kn, vt = _qkv_prep(proj, jnp.tile(b_q_norm[0:1], (1, 2)),
                           jnp.tile(b_k_norm[0:1], (1, 2)), batch, seq)
    yb = _attn(qt, kn, vt, proj, b_lambda[0], b_subln[0:1], batch, seq)
    out = _out_proj(ya, yb, proj, x2, w_branch_a[0].astype(BF16),
                    w_branch_b[0].astype(BF16), w_out[0].astype(BF16))
    return out.reshape(batch, seq, d)
```
